```python
import math
import jax, jax.numpy as jnp
from jax import lax
import numpy as np

D_MODEL = 1024
BATCH = 8
SEQ = 8192
DEPTH = 2
DEC_BATCH = 1
DEC_SEQ = 16384
PAST_LEN = 128

N_EVEN = (DEPTH + 1) // 2
N_ODD = DEPTH // 2
HEAD_DIM = 64
BLOCK = 128
POOL_WIDTH = D_MODEL // 2
POOL_WINDOWS = (2, 4, 8, 16)
POOL_GROUPS = len(POOL_WINDOWS)
POOL_GW = POOL_WIDTH // POOL_GROUPS
WIN_Q_HEADS = (D_MODEL - POOL_WIDTH) // HEAD_DIM
WIN_KV_HEADS = 2
WIN_GROUP = WIN_Q_HEADS // WIN_KV_HEADS
WINDOW = 128
IN0_WIDTH = POOL_WIDTH + (WIN_Q_HEADS + 2 * WIN_KV_HEADS) * HEAD_DIM
MIX0_OUT = POOL_WIDTH + WIN_Q_HEADS * HEAD_DIM
DIFF_HEADS = D_MODEL // (2 * HEAD_DIM)
DIFF_VDIM = 2 * HEAD_DIM
DIFF_QK = DIFF_HEADS * 2 * HEAD_DIM
IN1_WIDTH = 3 * DIFF_QK
MIX1_OUT = DIFF_HEADS * DIFF_VDIM
NUM_BUCKETS = 32
MAX_DISTANCE = 128
BIAS_HEADS = WIN_Q_HEADS
D_FF = 2816
N_EXPERTS = 8
TOP_K = 2
D_FF_EXPERT = 2816
EPS = 1e-6
NEG = -1e30

kernel_name = 'hybrid_pool_window_diff_moe_encoder'


def rmsnorm(x, g):
    xf = x.astype(jnp.float32)
    y = xf * lax.rsqrt(jnp.mean(xf * xf, axis=-1, keepdims=True) + EPS)
    return (y * g.astype(jnp.float32)).astype(x.dtype)


def t5_bucket(rel):
    half = NUM_BUCKETS // 2
    max_exact = half // 2
    ret = jnp.where(rel > 0, half, 0)
    n = jnp.abs(rel)
    nf = jnp.maximum(n, 1).astype(jnp.float32)
    large = max_exact + (jnp.log(nf / max_exact) / math.log(MAX_DISTANCE / max_exact)
                         * (half - max_exact)).astype(jnp.int32)
    large = jnp.minimum(large, half - 1)
    return ret + jnp.where(n < max_exact, n, large)


def swiglu(h, wg, wu, wd):
    return (jax.nn.silu(h @ wg) * (h @ wu)) @ wd


def pool_mixer(u, pool_w, pool_scale):
    S = u.shape[1]
    uf = u.astype(jnp.float32)
    cs = jnp.concatenate([jnp.zeros_like(uf[:, :1]), jnp.cumsum(uf, axis=1)], axis=1)
    t = jnp.arange(S)
    outs = []
    for g, w in enumerate(POOL_WINDOWS):
        half = w // 2
        lo = jnp.maximum(t - half, 0)
        hi = jnp.minimum(t + half - 1, S - 1)
        sl = slice(g * POOL_GW, (g + 1) * POOL_GW)
        csg = cs[..., sl]
        cnt = (hi - lo + 1).astype(jnp.float32)[None, :, None]
        d = ((csg[:, hi + 1] - csg[:, lo]) / cnt - uf[..., sl]).astype(u.dtype)
        outs.append(jnp.einsum('bsc,cd->bsd', d, pool_w[g]))
    return jnp.concatenate(outs, axis=-1) * pool_scale


def window_attention(q, k, v, sink, rel_bias):
    B, S = q.shape[0], q.shape[1]
    nb = S // BLOCK
    pad = ((0, 0), (BLOCK, BLOCK), (0, 0), (0, 0))
    kp = jnp.pad(k, pad)
    vp = jnp.pad(v, pad)
    rel = jnp.arange(3 * BLOCK)[None, :] - BLOCK - jnp.arange(BLOCK)[:, None]
    bias = rel_bias[t5_bucket(rel)].reshape(BLOCK, 3 * BLOCK, WIN_KV_HEADS, WIN_GROUP)
    bias = bias.transpose(2, 3, 0, 1).astype(jnp.float32)
    band = jnp.abs(rel) <= WINDOW
    sink_l = sink.reshape(WIN_KV_HEADS, WIN_GROUP).astype(jnp.float32)
    scale = HEAD_DIM ** -0.5

    def block(ib):
        q0 = ib * BLOCK
        qb = lax.dynamic_slice_in_dim(q, q0, BLOCK, axis=1)
        kb = lax.dynamic_slice_in_dim(kp, q0, 3 * BLOCK, axis=1)
        vb = lax.dynamic_slice_in_dim(vp, q0, 3 * BLOCK, axis=1)
        kpos = q0 - BLOCK + jnp.arange(3 * BLOCK)
        valid = band & ((kpos >= 0) & (kpos < S))[None, :]
        s = jnp.einsum('bqgrd,bkgd->bgrqk', qb, kb).astype(jnp.float32) * scale + bias
        s = jnp.where(valid, s, NEG)
        sk = jnp.broadcast_to(sink_l[None, :, :, None, None], s.shape[:-1] + (1,))
        p = jax.nn.softmax(jnp.concatenate([s, sk], axis=-1), axis=-1)[..., :-1]
        return jnp.einsum('bgrqk,bkgd->bqgrd', p.astype(vb.dtype), vb)

    out = lax.map(block, jnp.arange(nb))
    return out.transpose(1, 0, 2, 3, 4, 5).reshape(B, S, WIN_Q_HEADS * HEAD_DIM)


def mixer_even(h, w_in, pool_w, pool_scale, sink, w_out, rel_bias):
    B, S, _ = h.shape
    z = h @ w_in
    o = POOL_WIDTH
    u = z[..., :o]
    q = z[..., o:o + WIN_Q_HEADS * HEAD_DIM].reshape(B, S, WIN_KV_HEADS, WIN_GROUP, HEAD_DIM)
    o += WIN_Q_HEADS * HEAD_DIM
    k = z[..., o:o + WIN_KV_HEADS * HEAD_DIM].reshape(B, S, WIN_KV_HEADS, HEAD_DIM)
    o += WIN_KV_HEADS * HEAD_DIM
    v = z[..., o:].reshape(B, S, WIN_KV_HEADS, HEAD_DIM)
    a = pool_mixer(u, pool_w, pool_scale)
    b = window_attention(q, k, v, sink, rel_bias)
    return jnp.concatenate([a, b], axis=-1) @ w_out


def mixer_odd(h, w_in, lam_p, subln_g, w_out, rel_bias, lambda_init):
    B, S, _ = h.shape
    nb = S // BLOCK
    z = h @ w_in
    q = z[..., :DIFF_QK].reshape(B, S, DIFF_HEADS, 2, HEAD_DIM)
    k = z[..., DIFF_QK:2 * DIFF_QK].reshape(B, S, DIFF_HEADS, 2, HEAD_DIM)
    v = z[..., 2 * DIFF_QK:].reshape(B, S, DIFF_HEADS, DIFF_VDIM)
    lp = lam_p.astype(jnp.float32)
    lam = jnp.exp(jnp.sum(lp[0] * lp[1])) - jnp.exp(jnp.sum(lp[2] * lp[3])) + lambda_init
    kpos = jnp.arange(S)
    scale = HEAD_DIM ** -0.5

    def block(ib):
        q0 = ib * BLOCK
        qb = lax.dynamic_slice_in_dim(q, q0, BLOCK, axis=1)
        rel = kpos[None, :] - (q0 + jnp.arange(BLOCK))[:, None]
        bias = rel_bias[t5_bucket(rel)].transpose(2, 0, 1).astype(jnp.float32)
        s = jnp.einsum('bqhmd,bkhmd->bhmqk', qb, k).astype(jnp.float32) * scale + bias[None, :, None]
        p = jax.nn.softmax(s, axis=-1)
        a = p[:, :, 0] - lam * p[:, :, 1]
        return jnp.einsum('bhqk,bkhe->bqhe', a.astype(v.dtype), v)

    out = lax.map(block, jnp.arange(nb))
    out = out.transpose(1, 0, 2, 3, 4).reshape(B, S, DIFF_HEADS, DIFF_VDIM)
    out = rmsnorm(out, subln_g) * (1.0 - lambda_init)
    return out.reshape(B, S, MIX1_OUT) @ w_out


def moe(h, router_w, router_b, wg, wu, wd):
    B, S, D = h.shape
    xf = h.reshape(B * S, D)
    logits = (xf @ router_w).astype(jnp.float32) + router_b.astype(jnp.float32)
    top_v, top_i = lax.top_k(logits, TOP_K)
    top_w = jax.nn.softmax(top_v, axis=-1)
    gates = jnp.sum(jax.nn.one_hot(top_i, N_EXPERTS, dtype=jnp.float32) * top_w[..., None], axis=1)
    gates = gates.astype(xf.dtype)
    y = jnp.zeros_like(xf)
    for e in range(N_EXPERTS):
        y = y + gates[:, e:e + 1] * swiglu(xf, wg[e], wu[e], wd[e])
    return y.reshape(B, S, D)


def trunk(x, norm_mix, norm_ffn, rel_bias, w_in0, pool_w, pool_scale, sink, w_out0,
          ffn_wg, ffn_wu, ffn_wd, w_in1, diff_lambda, subln_g, w_out1,
          router_w, router_b, moe_wg, moe_wu, moe_wd):
    for layer in range(DEPTH):
        j = layer // 2
        h = rmsnorm(x, norm_mix[layer, 0])
        if layer % 2 == 0:
            m = mixer_even(h, w_in0[j], pool_w[j], pool_scale[j], sink[j], w_out0[j], rel_bias)
        else:
            lambda_init = 0.8 - 0.6 * math.exp(-0.3 * layer)
            m = mixer_odd(h, w_in1[j], diff_lambda[j], subln_g[j], w_out1[j], rel_bias, lambda_init)
        x = x + rmsnorm(m, norm_mix[layer, 1])
        h = rmsnorm(x, norm_ffn[layer, 0])
        if layer % 2 == 0:
            f = swiglu(h, ffn_wg[j], ffn_wu[j], ffn_wd[j])
        else:
            f = moe(h, router_w[j], router_b[j], moe_wg[j], moe_wu[j], moe_wd[j])
        x = x + rmsnorm(f, norm_ffn[layer, 1])
    return x


def setup_inputs(seed: int = 0) -> dict:
    key = jax.random.key(seed)
    ks = jax.random.split(key, 24)
    f32 = jnp.float32

    def dense(k, shape, fan_in):
        return jax.random.normal(k, shape, f32) * (fan_in ** -0.5)

    def gain(k, shape):
        return 1.0 + 0.05 * jax.random.normal(k, shape, f32)

    return {
        'x_prompt': jax.random.normal(ks[0], (BATCH, SEQ, D_MODEL), f32),
        'x_sample': jax.random.normal(ks[1], (DEC_BATCH, DEC_SEQ, D_MODEL), f32),
        'norm_mix': gain(ks[2], (DEPTH, 2, D_MODEL)),
        'norm_ffn': gain(ks[3], (DEPTH, 2, D_MODEL)),
        'rel_bias': 0.5 * jax.random.normal(ks[4], (NUM_BUCKETS, BIAS_HEADS), f32),
        'w_in0': dense(ks[5], (N_EVEN, D_MODEL, IN0_WIDTH), D_MODEL),
        'pool_w': dense(ks[6], (N_EVEN, POOL_GROUPS, POOL_GW, POOL_GW), POOL_GW),
        'pool_scale': gain(ks[7], (N_EVEN, POOL_WIDTH)),
        'sink': 0.5 * jax.random.normal(ks[8], (N_EVEN, WIN_Q_HEADS), f32),
        'w_out0': dense(ks[9], (N_EVEN, MIX0_OUT, D_MODEL), MIX0_OUT),
        'ffn_wg': dense(ks[10], (N_EVEN, D_MODEL, D_FF), D_MODEL),
        'ffn_wu': dense(ks[11], (N_EVEN, D_MODEL, D_FF), D_MODEL),
        'ffn_wd': dense(ks[12], (N_EVEN, D_FF, D_MODEL), D_FF),
        'w_in1': dense(ks[13], (N_ODD, D_MODEL, IN1_WIDTH), D_MODEL),
        'diff_lambda': 0.1 * jax.random.normal(ks[14], (N_ODD, 4, HEAD_DIM), f32),
        'subln_g': gain(ks[15], (N_ODD, DIFF_VDIM)),
        'w_out1': dense(ks[16], (N_ODD, MIX1_OUT, D_MODEL), MIX1_OUT),
        'router_w': dense(ks[17], (N_ODD, D_MODEL, N_EXPERTS), D_MODEL),
        'router_b': 0.01 * jax.random.normal(ks[18], (N_ODD, N_EXPERTS), f32),
        'moe_wg': dense(ks[19], (N_ODD, N_EXPERTS, D_MODEL, D_FF_EXPERT), D_MODEL),
        'moe_wu': dense(ks[20], (N_ODD, N_EXPERTS, D_MODEL, D_FF_EXPERT), D_MODEL),
        'moe_wd': dense(ks[21], (N_ODD, N_EXPERTS, D_FF_EXPERT, D_MODEL), D_FF_EXPERT),
    }


def reference(x_prompt, x_sample, norm_mix, norm_ffn, rel_bias, w_in0, pool_w, pool_scale, sink, w_out0,
              ffn_wg, ffn_wu, ffn_wd, w_in1, diff_lambda, subln_g, w_out1,
              router_w, router_b, moe_wg, moe_wu, moe_wd):
    y_prompt = trunk(x_prompt, norm_mix, norm_ffn, rel_bias, w_in0, pool_w, pool_scale, sink, w_out0,
                     ffn_wg, ffn_wu, ffn_wd, w_in1, diff_lambda, subln_g, w_out1,
                     router_w, router_b, moe_wg, moe_wu, moe_wd)
    y_sample = trunk(x_sample, norm_mix, norm_ffn, rel_bias, w_in0, pool_w, pool_scale, sink, w_out0,
                     ffn_wg, ffn_wu, ffn_wd, w_in1, diff_lambda, subln_g, w_out1,
                     router_w, router_b, moe_wg, moe_wu, moe_wd)
    return (y_prompt, y_sample)
```

```python
import functools
import math

import jax
import jax.numpy as jnp
from jax import lax
from jax.experimental import pallas as pl
from jax.experimental.pallas import tpu as pltpu

D_MODEL = 1024
HEAD_DIM = 64
POOL_WIDTH = 512
POOL_WINDOWS = (2, 4, 8, 16)
POOL_GW = 128
POOL_HALO = 8
WIN_Q_HEADS = 8
WIN_KV_HEADS = 2
WIN_GROUP = 4
WINDOW = 128
DIFF_HEADS = 8
DIFF_VDIM = 128
NUM_BUCKETS = 32
MAX_DISTANCE = 128
D_FF = 2816
N_EXPERTS = 8
EPS = 1e-6
NEG = -1e30

LANES = 128
VMEM_LIMIT = 56 * 1024 * 1024

TM = 512
TF = 1408
TQ_WIN = 256
T_DIFF = 256

F32 = jnp.float32
BF16 = jnp.bfloat16


def _cparams(*sem):
    return pltpu.CompilerParams(dimension_semantics=sem, vmem_limit_bytes=VMEM_LIMIT)


def _rms(xf, g):
    ms = jnp.mean(xf * xf, axis=-1, keepdims=True)
    return xf * lax.rsqrt(ms + EPS) * g


def _t5_bucket(rel):
    half = NUM_BUCKETS // 2
    max_exact = half // 2
    ret = jnp.where(rel > 0, half, 0)
    n = jnp.abs(rel)
    nf = jnp.maximum(n, 1).astype(jnp.float32)
    large = max_exact + (jnp.log(nf / max_exact) / math.log(MAX_DISTANCE / max_exact)
                         * (half - max_exact)).astype(jnp.int32)
    large = jnp.minimum(large, half - 1)
    return ret + jnp.where(n < max_exact, n, large)


def _norm_matmul_kernel(x_ref, g_ref, w_ref, *o_refs, splits):
    h = _rms(x_ref[...], g_ref[...]).astype(BF16)
    z = jnp.dot(h, w_ref[...], preferred_element_type=F32)
    for o_ref, (a, b) in zip(o_refs, splits):
        o_ref[...] = z[:, a:b].astype(o_ref.dtype)


def _norm_matmul(x, g, w, splits, dtypes):
    n, d = x.shape
    nout = w.shape[1]
    return pl.pallas_call(
        functools.partial(_norm_matmul_kernel, splits=splits),
        grid=(n // TM,),
        in_specs=[pl.BlockSpec((TM, d), lambda i: (i, 0)),
                  pl.BlockSpec((1, d), lambda i: (0, 0)),
                  pl.BlockSpec((d, nout), lambda i: (0, 0))],
        out_specs=[pl.BlockSpec((TM, b - a), lambda i: (i, 0)) for a, b in splits],
        out_shape=[jax.ShapeDtypeStruct((n, b - a), dt) for (a, b), dt in zip(splits, dtypes)],
        compiler_params=_cparams("parallel"),
        name="norm_matmul",
    )(x, g.reshape(1, d), w)


def _proj_norm_res_kernel(*refs, n_in):
    ins, ws = refs[:n_in], refs[n_in:2 * n_in]
    x_ref, g_ref, o_ref = refs[2 * n_in:]
    m = jnp.dot(ins[0][...], ws[0][...], preferred_element_type=F32)
    for a_ref, w_ref in zip(ins[1:], ws[1:]):
        m = m + jnp.dot(a_ref[...], w_ref[...], preferred_element_type=F32)
    o_ref[...] = x_ref[...] + _rms(m, g_ref[...])


def _proj_norm_res(ins, ws, x, g):
    n, d = x.shape
    n_in = len(ins)
    return pl.pallas_call(
        functools.partial(_proj_norm_res_kernel, n_in=n_in),
        grid=(n // TM,),
        in_specs=([pl.BlockSpec((TM, a.shape[1]), lambda i: (i, 0)) for a in ins]
                  + [pl.BlockSpec(w.shape, lambda i: (0, 0)) for w in ws]
                  + [pl.BlockSpec((TM, d), lambda i: (i, 0)),
                     pl.BlockSpec((1, d), lambda i: (0, 0))]),
        out_specs=pl.BlockSpec((TM, d), lambda i: (i, 0)),
        out_shape=jax.ShapeDtypeStruct((n, d), F32),
        compiler_params=_cparams("parallel"),
        name="proj_norm_res",
    )(*ins, *ws, x, g.reshape(1, d))


def _pool_kernel(up_ref, u_ref, un_ref, w_ref, sc_ref, o_ref, ext_ref, *, seq_len):
    tm = u_ref.shape[0]
    t0 = (pl.program_id(0) * tm) % seq_len
    ext_ref[0:POOL_HALO, :] = jnp.where(t0 > 0, up_ref[...], 0.0)
    ext_ref[POOL_HALO:POOL_HALO + tm, :] = u_ref[...]
    ext_ref[POOL_HALO + tm:, :] = jnp.where(t0 + tm < seq_len, un_ref[...], 0.0)
    t = t0 + lax.broadcasted_iota(jnp.int32, (tm, 1), 0)
    outs = []
    for g, w in enumerate(POOL_WINDOWS):
        half = w // 2
        sl = slice(g * POOL_GW, (g + 1) * POOL_GW)
        acc = ext_ref[POOL_HALO - half:POOL_HALO - half + tm, sl]
        for j in range(-half + 1, half):
            acc = acc + ext_ref[POOL_HALO + j:POOL_HALO + j + tm, sl]
        lo = jnp.maximum(t - half, 0)
        hi = jnp.minimum(t + half - 1, seq_len - 1)
        cnt = (hi - lo + 1).astype(F32)
        d = (acc / cnt - u_ref[:, sl]).astype(BF16)
        outs.append(jnp.dot(d, w_ref[g], preferred_element_type=F32))
    o_ref[...] = (jnp.concatenate(outs, axis=-1) * sc_ref[...]).astype(o_ref.dtype)


def _pool_mixer(u, pool_w, pool_scale, seq_len):
    n, c = u.shape
    hb = TM // POOL_HALO
    last = n // POOL_HALO - 1
    return pl.pallas_call(
        functools.partial(_pool_kernel, seq_len=seq_len),
        grid=(n // TM,),
        in_specs=[pl.BlockSpec((POOL_HALO, c), lambda i: (jnp.maximum(i * hb - 1, 0), 0)),
                  pl.BlockSpec((TM, c), lambda i: (i, 0)),
                  pl.BlockSpec((POOL_HALO, c), lambda i: (jnp.minimum((i + 1) * hb, last), 0)),
                  pl.BlockSpec(pool_w.shape, lambda i: (0, 0, 0)),
                  pl.BlockSpec((1, c), lambda i: (0, 0))],
        out_specs=pl.BlockSpec((TM, c), lambda i: (i, 0)),
        out_shape=jax.ShapeDtypeStruct((n, c), BF16),
        scratch_shapes=[pltpu.VMEM((TM + 2 * POOL_HALO, c), F32)],
        compiler_params=_cparams("parallel"),
        name="pool_mixer",
    )(u, u, u, pool_w, pool_scale.reshape(1, c))


def _win_kernel(q_ref, kvp_ref, kv_ref, kvn_ref, bias_ref, sink_ref, o_ref, *, seq_len):
    tq = q_ref.shape[0]
    nk = tq + 2 * WINDOW
    t0 = (pl.program_id(0) * tq) % seq_len
    kv = jnp.concatenate([kvp_ref[...], kv_ref[...], kvn_ref[...]], axis=0)
    kpos = t0 - WINDOW + lax.broadcasted_iota(jnp.int32, (1, nk), 1)
    valid = (kpos >= 0) & (kpos < seq_len)
    q = q_ref[...]
    sink = sink_ref[...]
    outs = []
    for g in range(WIN_KV_HEADS):
        k = kv[:, g * HEAD_DIM:(g + 1) * HEAD_DIM]
        v = kv[:, (WIN_KV_HEADS + g) * HEAD_DIM:(WIN_KV_HEADS + g + 1) * HEAD_DIM]
        for r in range(WIN_GROUP):
            h = g * WIN_GROUP + r
            qh = q[:, h * HEAD_DIM:(h + 1) * HEAD_DIM]
            s = lax.dot_general(qh, k, (((1,), (1,)), ((), ())), preferred_element_type=F32)
            s = jnp.where(valid, s + bias_ref[h], NEG)
            sk = sink[:, h:h + 1]
            m = jnp.maximum(jnp.max(s, axis=-1, keepdims=True), sk)
            e = jnp.exp(s - m)
            l = jnp.sum(e, axis=-1, keepdims=True) + jnp.exp(sk - m)
            p = (e / l).astype(BF16)
            outs.append(jnp.dot(p, v, preferred_element_type=F32))
    o_ref[...] = jnp.concatenate(outs, axis=-1).astype(o_ref.dtype)


def _window_bias(rel_bias, tq):
    nk = tq + 2 * WINDOW
    rel = jnp.arange(nk)[None, :] - WINDOW - jnp.arange(tq)[:, None]
    bias = rel_bias[_t5_bucket(rel)].transpose(2, 0, 1).astype(F32)
    return jnp.where((jnp.abs(rel) <= WINDOW)[None], bias, NEG)


def _window_attention(q, kv, rel_bias, sink, seq_len):
    n = q.shape[0]
    tq = TQ_WIN
    hb = tq // WINDOW
    last = n // WINDOW - 1
    bias = _window_bias(rel_bias, tq)
    return pl.pallas_call(
        functools.partial(_win_kernel, seq_len=seq_len),
        grid=(n // tq,),
        in_specs=[pl.BlockSpec((tq, q.shape[1]), lambda i: (i, 0)),
                  pl.BlockSpec((WINDOW, kv.shape[1]), lambda i: (jnp.maximum(i * hb - 1, 0), 0)),
                  pl.BlockSpec((tq, kv.shape[1]), lambda i: (i, 0)),
                  pl.BlockSpec((WINDOW, kv.shape[1]), lambda i: (jnp.minimum((i + 1) * hb, last), 0)),
                  pl.BlockSpec(bias.shape, lambda i: (0, 0, 0)),
                  pl.BlockSpec((1, WIN_Q_HEADS), lambda i: (0, 0))],
        out_specs=pl.BlockSpec((tq, q.shape[1]), lambda i: (i, 0)),
        out_shape=jax.ShapeDtypeStruct(q.shape, BF16),
        compiler_params=_cparams("parallel"),
        name="window_attention",
    )(q, kv, kv, kv, bias, sink.reshape(1, WIN_Q_HEADS).astype(F32))


def _ffn_kernel(x_ref, gi_ref, wg_ref, wu_ref, wd_ref, go_ref, o_ref, h_ref, acc_ref):
    j = pl.program_id(1)

    @pl.when(j == 0)
    def _():
        h_ref[...] = _rms(x_ref[...], gi_ref[...]).astype(BF16)
        acc_ref[...] = jnp.zeros_like(acc_ref)

    h = h_ref[...]
    a = jnp.dot(h, wg_ref[...], preferred_element_type=F32)
    b = jnp.dot(h, wu_ref[...], preferred_element_type=F32)
    mid = (a * jax.nn.sigmoid(a) * b).astype(BF16)
    acc_ref[...] += jnp.dot(mid, wd_ref[...], preferred_element_type=F32)

    @pl.when(j == pl.num_programs(1) - 1)
    def _():
        o_ref[...] = x_ref[...] + _rms(acc_ref[...], go_ref[...])


def _ffn(x, g_in, wg, wu, wd, g_out):
    n, d = x.shape
    nf = wg.shape[1] // TF
    return pl.pallas_call(
        _ffn_kernel,
        grid=(n // TM, nf),
        in_specs=[pl.BlockSpec((TM, d), lambda i, j: (i, 0)),
                  pl.BlockSpec((1, d), lambda i, j: (0, 0)),
                  pl.BlockSpec((d, TF), lambda i, j: (0, j)),
                  pl.BlockSpec((d, TF), lambda i, j: (0, j)),
                  pl.BlockSpec((TF, d), lambda i, j: (j, 0)),
                  pl.BlockSpec((1, d), lambda i, j: (0, 0))],
        out_specs=pl.BlockSpec((TM, d), lambda i, j: (i, 0)),
        out_shape=jax.ShapeDtypeStruct((n, d), F32),
        scratch_shapes=[pltpu.VMEM((TM, d), BF16), pltpu.VMEM((TM, d), F32)],
        compiler_params=_cparams("parallel", "arbitrary"),
        name="ffn",
    )(x, g_in.reshape(1, d), wg, wu, wd, g_out.reshape(1, d))


def _diff_kernel(qt_ref, k_ref, vt_ref, bias_ref, lam_ref, g_ref, o_ref,
                 qbd_ref, acc_ref, m_ref, l_ref, *, lambda_init):
    t = qt_ref.shape[1]
    nk = k_ref.shape[0] // t
    i = pl.program_id(2)
    qt = qt_ref[...]
    zero = jnp.zeros((HEAD_DIM, t), BF16)
    qbd_ref[0:HEAD_DIM, 0:t] = qt[0:HEAD_DIM]
    qbd_ref[0:HEAD_DIM, t:] = zero
    qbd_ref[HEAD_DIM:, 0:t] = zero
    qbd_ref[HEAD_DIM:, t:] = qt[HEAD_DIM:]
    acc_ref[...] = jnp.zeros_like(acc_ref)
    m_ref[...] = jnp.full_like(m_ref, NEG)
    l_ref[...] = jnp.zeros_like(l_ref)

    def body(j, carry):
        kc = k_ref[pl.ds(pl.multiple_of(j * t, t), t), :]
        s = jnp.dot(kc, qbd_ref[...], preferred_element_type=F32)
        b = bias_ref[jnp.clip(j - i, -2, 2) + 2]
        s = s + jnp.concatenate([b, b], axis=1)
        m_old = m_ref[...]
        m_new = jnp.maximum(m_old, jnp.max(s, axis=0, keepdims=True))
        alpha = jnp.exp(m_old - m_new)
        e = jnp.exp(s - m_new)
        l_ref[...] = alpha * l_ref[...] + jnp.sum(e, axis=0, keepdims=True)
        pv = jnp.dot(vt_ref[j], e.astype(BF16), preferred_element_type=F32)
        acc_ref[...] = alpha * acc_ref[...] + pv
        m_ref[...] = m_new
        return carry

    lax.fori_loop(0, nk, body, 0)

    o = acc_ref[...] / l_ref[...]
    lp = lam_ref[...]
    lam = (jnp.exp(jnp.sum(lp[0:1] * lp[1:2], axis=-1, keepdims=True))
           - jnp.exp(jnp.sum(lp[2:3] * lp[3:4], axis=-1, keepdims=True)) + lambda_init)
    out = o[:, :t] - lam * o[:, t:]
    ms = jnp.mean(out * out, axis=0, keepdims=True)
    y = out * lax.rsqrt(ms + EPS) * g_ref[...] * (1.0 - lambda_init)
    o_ref[...] = y.T.astype(o_ref.dtype)


def _diff_bias(rel_bias, t):
    kk = jnp.arange(t)[:, None]
    qq = jnp.arange(t)[None, :]
    tiles = []
    for d in range(-2, 3):
        rel = d * t + kk - qq
        tiles.append(rel_bias[_t5_bucket(rel)].transpose(2, 0, 1).astype(F32))
    return jnp.stack(tiles, axis=1)


def _diff_attention(qt, k, vt, bias, diff_lambda, subln_g, seq_len, lambda_init):
    n = k.shape[0]
    t = T_DIFF
    assert t >= MAX_DISTANCE and seq_len % t == 0
    nb = n // seq_len
    nq = seq_len // t
    return pl.pallas_call(
        functools.partial(_diff_kernel, lambda_init=lambda_init),
        grid=(nb, DIFF_HEADS, nq),
        in_specs=[pl.BlockSpec((DIFF_VDIM, t), lambda b, h, i: (h, b * nq + i)),
                  pl.BlockSpec((seq_len, DIFF_VDIM), lambda b, h, i: (b, h)),
                  pl.BlockSpec((nq, DIFF_VDIM, t), lambda b, h, i: (b, h, 0)),
                  pl.BlockSpec((None, 5, t, t), lambda b, h, i: (h, 0, 0, 0)),
                  pl.BlockSpec((4, HEAD_DIM), lambda b, h, i: (0, 0)),
                  pl.BlockSpec((DIFF_VDIM, 1), lambda b, h, i: (0, 0))],
        out_specs=pl.BlockSpec((t, DIFF_VDIM), lambda b, h, i: (b * nq + i, h)),
        out_shape=jax.ShapeDtypeStruct((n, DIFF_HEADS * DIFF_VDIM), BF16),
        scratch_shapes=[pltpu.VMEM((DIFF_VDIM, 2 * t), BF16),
                        pltpu.VMEM((DIFF_VDIM, 2 * t), F32),
                        pltpu.VMEM((1, 2 * t), F32),
                        pltpu.VMEM((1, 2 * t), F32)],
        compiler_params=_cparams("parallel", "parallel", "arbitrary"),
        name="diff_attention",
    )(qt, k, vt, bias, diff_lambda.astype(F32), subln_g.reshape(DIFF_VDIM, 1).astype(F32))


def _moe_kernel(x_ref, gi_ref, rw_ref, rb_ref, wg_ref, wu_ref, wd_ref, go_ref, o_ref,
                h_ref, gate_ref, acc_ref):
    e = pl.program_id(1)
    j = pl.program_id(2)
    col = lax.broadcasted_iota(jnp.int32, gate_ref.shape, 1)

    @pl.when((e == 0) & (j == 0))
    def _():
        h = _rms(x_ref[...], gi_ref[...]).astype(BF16)
        h_ref[...] = h
        acc_ref[...] = jnp.zeros_like(acc_ref)
        logits = jnp.dot(h, rw_ref[...], preferred_element_type=F32) + rb_ref[...]
        m1 = jnp.max(logits, axis=-1, keepdims=True)
        i1 = jnp.min(jnp.where(logits == m1, col, LANES), axis=-1, keepdims=True)
        rest = jnp.where(col == i1, 2 * NEG, logits)
        m2 = jnp.max(rest, axis=-1, keepdims=True)
        i2 = jnp.min(jnp.where(rest == m2, col, LANES), axis=-1, keepdims=True)
        e2 = jnp.exp(m2 - m1)
        den = 1.0 + e2
        gate_ref[...] = jnp.where(col == i1, 1.0 / den, 0.0) + jnp.where(col == i2, e2 / den, 0.0)

    h = h_ref[...]
    a = jnp.dot(h, wg_ref[...], preferred_element_type=F32)
    b = jnp.dot(h, wu_ref[...], preferred_element_type=F32)
    mid = (a * jax.nn.sigmoid(a) * b).astype(BF16)
    y = jnp.dot(mid, wd_ref[...], preferred_element_type=F32)
    gate = jnp.sum(jnp.where(col == e, gate_ref[...], 0.0), axis=-1, keepdims=True)
    acc_ref[...] += gate * y

    @pl.when((e == pl.num_programs(1) - 1) & (j == pl.num_programs(2) - 1))
    def _():
        o_ref[...] = x_ref[...] + _rms(acc_ref[...], go_ref[...])


def _moe(x, g_in, router_w, router_b, wg, wu, wd, g_out):
    n, d = x.shape
    ne, _, dff = wg.shape
    nf = dff // TF
    rw = jnp.zeros((d, LANES), BF16).at[:, :ne].set(router_w.astype(BF16))
    rb = jnp.full((1, LANES), NEG, F32).at[0, :ne].set(router_b.astype(F32))
    return pl.pallas_call(
        _moe_kernel,
        grid=(n // TM, ne, nf),
        in_specs=[pl.BlockSpec((TM, d), lambda i, e, j: (i, 0)),
                  pl.BlockSpec((1, d), lambda i, e, j: (0, 0)),
                  pl.BlockSpec((d, LANES), lambda i, e, j: (0, 0)),
                  pl.BlockSpec((1, LANES), lambda i, e, j: (0, 0)),
                  pl.BlockSpec((None, d, TF), lambda i, e, j: (e, 0, j)),
                  pl.BlockSpec((None, d, TF), lambda i, e, j: (e, 0, j)),
                  pl.BlockSpec((None, TF, d), lambda i, e, j: (e, j, 0)),
                  pl.BlockSpec((1, d), lambda i, e, j: (0, 0))],
        out_specs=pl.BlockSpec((TM, d), lambda i, e, j: (i, 0)),
        out_shape=jax.ShapeDtypeStruct((n, d), F32),
        scratch_shapes=[pltpu.VMEM((TM, d), BF16), pltpu.VMEM((TM, LANES), F32), pltpu.VMEM((TM, d), F32)],
        compiler_params=_cparams("parallel", "arbitrary", "arbitrary"),
        name="moe",
    )(x, g_in.reshape(1, d), rw, rb, wg, wu, wd, g_out.reshape(1, d))


def _trunk(x, seq_len, p):
    n = x.shape[0]
    depth = p["norm_mix"].shape[0]
    for layer in range(depth):
        j = layer // 2
        if layer % 2 == 0:
            o_q = POOL_WIDTH
            o_kv = o_q + WIN_Q_HEADS * HEAD_DIM
            o_end = o_kv + 2 * WIN_KV_HEADS * HEAD_DIM
            u, q, kv = _norm_matmul(x, p["norm_mix"][layer, 0], p["w_in0"][j],
                                    ((0, o_q), (o_q, o_kv), (o_kv, o_end)), (F32, BF16, BF16))
            a = _pool_mixer(u, p["pool_w"][j], p["pool_scale"][j], seq_len)
            b = _window_attention(q, kv, p["rel_bias"], p["sink"][j], seq_len)
            w_out = p["w_out0"][j]
            x = _proj_norm_res([a, b], [w_out[:POOL_WIDTH], w_out[POOL_WIDTH:]], x, p["norm_mix"][layer, 1])
            x = _ffn(x, p["norm_ffn"][layer, 0], p["ffn_wg"][j], p["ffn_wu"][j], p["ffn_wd"][j],
                     p["norm_ffn"][layer, 1])
        else:
            lambda_init = 0.8 - 0.6 * math.exp(-0.3 * layer)
            dq = DIFF_HEADS * 2 * HEAD_DIM
            q, k, v = _norm_matmul(x, p["norm_mix"][layer, 0], p["w_in1"][j],
                                   ((0, dq), (dq, 2 * dq), (2 * dq, 3 * dq)), (BF16, BF16, BF16))
            t = T_DIFF
            qt = q.T
            vt = v.reshape(n // t, t, dq).transpose(0, 2, 1)
            m = _diff_attention(qt, k, vt, p["diff_bias"], p["diff_lambda"][j], p["subln_g"][j],
                                seq_len, lambda_init)
            x = _proj_norm_res([m], [p["w_out1"][j]], x, p["norm_mix"][layer, 1])
            x = _moe(x, p["norm_ffn"][layer, 0], p["router_w"][j], p["router_b"][j],
                     p["moe_wg"][j], p["moe_wu"][j], p["moe_wd"][j], p["norm_ffn"][layer, 1])
    return x


def _prepare(norm_mix, norm_ffn, rel_bias, w_in0, pool_w, pool_scale, sink, w_out0,
             ffn_wg, ffn_wu, ffn_wd, w_in1, diff_lambda, subln_g, w_out1,
             router_w, router_b, moe_wg, moe_wu, moe_wd):
    scale = HEAD_DIM ** -0.5
    o_q = POOL_WIDTH
    o_kv = o_q + WIN_Q_HEADS * HEAD_DIM
    col0 = jnp.arange(w_in0.shape[-1])
    s0 = jnp.where((col0 >= o_q) & (col0 < o_kv), scale, 1.0).astype(F32)
    col1 = jnp.arange(w_in1.shape[-1])
    s1 = jnp.where(col1 < DIFF_HEADS * 2 * HEAD_DIM, scale, 1.0).astype(F32)
    return dict(
        norm_mix=norm_mix.astype(F32), norm_ffn=norm_ffn.astype(F32), rel_bias=rel_bias,
        diff_bias=_diff_bias(rel_bias, T_DIFF),
        w_in0=(w_in0 * s0).astype(BF16), pool_w=pool_w.astype(BF16), pool_scale=pool_scale.astype(F32),
        sink=sink, w_out0=w_out0.astype(BF16),
        ffn_wg=ffn_wg.astype(BF16), ffn_wu=ffn_wu.astype(BF16), ffn_wd=ffn_wd.astype(BF16),
        w_in1=(w_in1 * s1).astype(BF16), diff_lambda=diff_lambda, subln_g=subln_g,
        w_out1=w_out1.astype(BF16), router_w=router_w, router_b=router_b,
        moe_wg=moe_wg.astype(BF16), moe_wu=moe_wu.astype(BF16), moe_wd=moe_wd.astype(BF16))


def kernel(x_prompt, x_sample, norm_mix, norm_ffn, rel_bias, w_in0, pool_w, pool_scale, sink, w_out0,
           ffn_wg, ffn_wu, ffn_wd, w_in1, diff_lambda, subln_g, w_out1,
           router_w, router_b, moe_wg, moe_wu, moe_wd):
    p = _prepare(norm_mix, norm_ffn, rel_bias, w_in0, pool_w, pool_scale, sink, w_out0,
                 ffn_wg, ffn_wu, ffn_wd, w_in1, diff_lambda, subln_g, w_out1,
                 router_w, router_b, moe_wg, moe_wu, moe_wd)
    outs = []
    for x in (x_prompt, x_sample):
        b, s, d = x.shape
        outs.append(_trunk(x.reshape(b * s, d), s, p).reshape(b, s, d))
    return tuple(outs)
```

```python
import functools
import math

import jax
import jax.numpy as jnp
from jax import lax
from jax.experimental import pallas as pl
from jax.experimental.pallas import tpu as pltpu

D_MODEL = 1024
HEAD_DIM = 64
POOL_WIDTH = 512
POOL_WINDOWS = (2, 4, 8, 16)
POOL_GW = 128
POOL_HALO = 8
WIN_Q_HEADS = 8
WIN_KV_HEADS = 2
WIN_GROUP = 4
WINDOW = 128
DIFF_HEADS = 8
DIFF_VDIM = 128
NUM_BUCKETS = 32
MAX_DISTANCE = 128
D_FF = 2816
N_EXPERTS = 8
EPS = 1e-6
NEG = -1e30
LOG2E = 1.4426950408889634

LANES = 128
BF16_SUBLANES = 16
VMEM_LIMIT = 56 * 1024 * 1024

TM = 512
TF = 1408
TQ_WIN = 256
TQ_DIFF = 256
TK_DIFF = 512
TB_DIFF = 256

F32 = jnp.float32
BF16 = jnp.bfloat16


def _cparams(*sem):
    return pltpu.CompilerParams(dimension_semantics=sem, vmem_limit_bytes=VMEM_LIMIT)


def _rms(xf, g):
    ms = jnp.mean(xf * xf, axis=-1, keepdims=True)
    return xf * lax.rsqrt(ms + EPS) * g


def _t5_bucket(rel):
    half = NUM_BUCKETS // 2
    max_exact = half // 2
    ret = jnp.where(rel > 0, half, 0)
    n = jnp.abs(rel)
    nf = jnp.maximum(n, 1).astype(jnp.float32)
    large = max_exact + (jnp.log(nf / max_exact) / math.log(MAX_DISTANCE / max_exact)
                         * (half - max_exact)).astype(jnp.int32)
    large = jnp.minimum(large, half - 1)
    return ret + jnp.where(n < max_exact, n, large)


def _bias_lookup(rel_bias, rel):
    onehot = (_t5_bucket(rel)[..., None] == jnp.arange(NUM_BUCKETS)).astype(F32)
    out = jnp.einsum("...b,bh->h...", onehot, rel_bias.astype(F32), precision=lax.Precision.HIGHEST)
    return out


def _norm_matmul_kernel(x_ref, g_ref, w_ref, *o_refs, splits):
    h = _rms(x_ref[...], g_ref[...]).astype(BF16)
    z = jnp.dot(h, w_ref[...], preferred_element_type=F32)
    for o_ref, (a, b) in zip(o_refs, splits):
        o_ref[...] = z[:, a:b].astype(o_ref.dtype)


def _norm_matmul(x, g, w, splits, dtypes):
    n, d = x.shape
    nout = w.shape[1]
    return pl.pallas_call(
        functools.partial(_norm_matmul_kernel, splits=splits),
        grid=(n // TM,),
        in_specs=[pl.BlockSpec((TM, d), lambda i: (i, 0)),
                  pl.BlockSpec((1, d), lambda i: (0, 0)),
                  pl.BlockSpec((d, nout), lambda i: (0, 0))],
        out_specs=[pl.BlockSpec((TM, b - a), lambda i: (i, 0)) for a, b in splits],
        out_shape=[jax.ShapeDtypeStruct((n, b - a), dt) for (a, b), dt in zip(splits, dtypes)],
        compiler_params=_cparams("parallel"),
        name="norm_matmul",
    )(x, g.reshape(1, d), w)


def _proj_norm_res_kernel(*refs, n_in):
    ins, ws = refs[:n_in], refs[n_in:2 * n_in]
    x_ref, g_ref, o_ref = refs[2 * n_in:]
    m = jnp.dot(ins[0][...], ws[0][...], preferred_element_type=F32)
    for a_ref, w_ref in zip(ins[1:], ws[1:]):
        m = m + jnp.dot(a_ref[...], w_ref[...], preferred_element_type=F32)
    o_ref[...] = x_ref[...] + _rms(m, g_ref[...])


def _proj_norm_res(ins, ws, x, g):
    n, d = x.shape
    n_in = len(ins)
    return pl.pallas_call(
        functools.partial(_proj_norm_res_kernel, n_in=n_in),
        grid=(n // TM,),
        in_specs=([pl.BlockSpec((TM, a.shape[1]), lambda i: (i, 0)) for a in ins]
                  + [pl.BlockSpec(w.shape, lambda i: (0, 0)) for w in ws]
                  + [pl.BlockSpec((TM, d), lambda i: (i, 0)),
                     pl.BlockSpec((1, d), lambda i: (0, 0))]),
        out_specs=pl.BlockSpec((TM, d), lambda i: (i, 0)),
        out_shape=jax.ShapeDtypeStruct((n, d), F32),
        compiler_params=_cparams("parallel"),
        name="proj_norm_res",
    )(*ins, *ws, x, g.reshape(1, d))


def _pool_kernel(up_ref, u_ref, un_ref, w_ref, sc_ref, o_ref, ext_ref, *, seq_len):
    tm = u_ref.shape[0]
    t0 = (pl.program_id(0) * tm) % seq_len
    ext_ref[0:POOL_HALO, :] = jnp.where(t0 > 0, up_ref[...], 0.0)
    ext_ref[POOL_HALO:POOL_HALO + tm, :] = u_ref[...]
    ext_ref[POOL_HALO + tm:, :] = jnp.where(t0 + tm < seq_len, un_ref[...], 0.0)
    t = t0 + lax.broadcasted_iota(jnp.int32, (tm, 1), 0)
    outs = []
    for g, w in enumerate(POOL_WINDOWS):
        half = w // 2
        sl = slice(g * POOL_GW, (g + 1) * POOL_GW)
        acc = ext_ref[POOL_HALO - half:POOL_HALO - half + tm, sl]
        for j in range(-half + 1, half):
            acc = acc + ext_ref[POOL_HALO + j:POOL_HALO + j + tm, sl]
        lo = jnp.maximum(t - half, 0)
        hi = jnp.minimum(t + half - 1, seq_len - 1)
        cnt = (hi - lo + 1).astype(F32)
        d = (acc / cnt - u_ref[:, sl]).astype(BF16)
        outs.append(jnp.dot(d, w_ref[g], preferred_element_type=F32))
    o_ref[...] = (jnp.concatenate(outs, axis=-1) * sc_ref[...]).astype(o_ref.dtype)


def _pool_mixer(u, pool_w, pool_scale, seq_len):
    n, c = u.shape
    hb = TM // POOL_HALO
    last = n // POOL_HALO - 1
    return pl.pallas_call(
        functools.partial(_pool_kernel, seq_len=seq_len),
        grid=(n // TM,),
        in_specs=[pl.BlockSpec((POOL_HALO, c), lambda i: (jnp.maximum(i * hb - 1, 0), 0)),
                  pl.BlockSpec((TM, c), lambda i: (i, 0)),
                  pl.BlockSpec((POOL_HALO, c), lambda i: (jnp.minimum((i + 1) * hb, last), 0)),
                  pl.BlockSpec(pool_w.shape, lambda i: (0, 0, 0)),
                  pl.BlockSpec((1, c), lambda i: (0, 0))],
        out_specs=pl.BlockSpec((TM, c), lambda i: (i, 0)),
        out_shape=jax.ShapeDtypeStruct((n, c), BF16),
        scratch_shapes=[pltpu.VMEM((TM + 2 * POOL_HALO, c), F32)],
        compiler_params=_cparams("parallel"),
        name="pool_mixer",
    )(u, u, u, pool_w, pool_scale.reshape(1, c))


def _win_kernel(q_ref, kvp_ref, kv_ref, kvn_ref, bias_ref, sink_ref, o_ref, *, seq_len):
    tq = q_ref.shape[0]
    nk = tq + 2 * WINDOW
    t0 = (pl.program_id(0) * tq) % seq_len
    kv = jnp.concatenate([kvp_ref[...], kv_ref[...], kvn_ref[...]], axis=0)
    kpos = t0 - WINDOW + lax.broadcasted_iota(jnp.int32, (1, nk), 1)
    valid = (kpos >= 0) & (kpos < seq_len)
    q = q_ref[...]
    sink = sink_ref[...]
    outs = []
    for g in range(WIN_KV_HEADS):
        k = kv[:, g * HEAD_DIM:(g + 1) * HEAD_DIM]
        v = kv[:, (WIN_KV_HEADS + g) * HEAD_DIM:(WIN_KV_HEADS + g + 1) * HEAD_DIM]
        for r in range(WIN_GROUP):
            h = g * WIN_GROUP + r
            qh = q[:, h * HEAD_DIM:(h + 1) * HEAD_DIM]
            s = lax.dot_general(qh, k, (((1,), (1,)), ((), ())), preferred_element_type=F32)
            s = jnp.where(valid, s + bias_ref[h], NEG)
            sk = sink[:, h:h + 1]
            m = jnp.maximum(jnp.max(s, axis=-1, keepdims=True), sk)
            e = jnp.exp(s - m)
            l = jnp.sum(e, axis=-1, keepdims=True) + jnp.exp(sk - m)
            p = (e / l).astype(BF16)
            outs.append(jnp.dot(p, v, preferred_element_type=F32))
    o_ref[...] = jnp.concatenate(outs, axis=-1).astype(o_ref.dtype)


def _window_bias(rel_bias, tq):
    nk = tq + 2 * WINDOW
    rel = jnp.arange(nk)[None, :] - WINDOW - jnp.arange(tq)[:, None]
    return jnp.where((jnp.abs(rel) <= WINDOW)[None], _bias_lookup(rel_bias, rel), NEG)


def _window_attention(q, kv, bias, sink, seq_len):
    n = q.shape[0]
    tq = TQ_WIN
    hb = tq // WINDOW
    last = n // WINDOW - 1
    return pl.pallas_call(
        functools.partial(_win_kernel, seq_len=seq_len),
        grid=(n // tq,),
        in_specs=[pl.BlockSpec((tq, q.shape[1]), lambda i: (i, 0)),
                  pl.BlockSpec((WINDOW, kv.shape[1]), lambda i: (jnp.maximum(i * hb - 1, 0), 0)),
                  pl.BlockSpec((tq, kv.shape[1]), lambda i: (i, 0)),
                  pl.BlockSpec((WINDOW, kv.shape[1]), lambda i: (jnp.minimum((i + 1) * hb, last), 0)),
                  pl.BlockSpec(bias.shape, lambda i: (0, 0, 0)),
                  pl.BlockSpec((1, WIN_Q_HEADS), lambda i: (0, 0))],
        out_specs=pl.BlockSpec((tq, q.shape[1]), lambda i: (i, 0)),
        out_shape=jax.ShapeDtypeStruct(q.shape, BF16),
        compiler_params=_cparams("parallel"),
        name="window_attention",
    )(q, kv, kv, kv, bias, sink.reshape(1, WIN_Q_HEADS).astype(F32))


def _ffn_kernel(x_ref, gi_ref, wg_ref, wu_ref, wd_ref, go_ref, o_ref, h_ref, acc_ref):
    j = pl.program_id(1)

    @pl.when(j == 0)
    def _():
        h_ref[...] = _rms(x_ref[...], gi_ref[...]).astype(BF16)
        acc_ref[...] = jnp.zeros_like(acc_ref)

    h = h_ref[...]
    a = jnp.dot(h, wg_ref[...], preferred_element_type=F32)
    b = jnp.dot(h, wu_ref[...], preferred_element_type=F32)
    mid = (a * jax.nn.sigmoid(a) * b).astype(BF16)
    acc_ref[...] += jnp.dot(mid, wd_ref[...], preferred_element_type=F32)

    @pl.when(j == pl.num_programs(1) - 1)
    def _():
        o_ref[...] = x_ref[...] + _rms(acc_ref[...], go_ref[...])


def _ffn(x, g_in, wg, wu, wd, g_out):
    n, d = x.shape
    nf = wg.shape[1] // TF
    return pl.pallas_call(
        _ffn_kernel,
        grid=(n // TM, nf),
        in_specs=[pl.BlockSpec((TM, d), lambda i, j: (i, 0)),
                  pl.BlockSpec((1, d), lambda i, j: (0, 0)),
                  pl.BlockSpec((d, TF), lambda i, j: (0, j)),
                  pl.BlockSpec((d, TF), lambda i, j: (0, j)),
                  pl.BlockSpec((TF, d), lambda i, j: (j, 0)),
                  pl.BlockSpec((1, d), lambda i, j: (0, 0))],
        out_specs=pl.BlockSpec((TM, d), lambda i, j: (i, 0)),
        out_shape=jax.ShapeDtypeStruct((n, d), F32),
        scratch_shapes=[pltpu.VMEM((TM, d), BF16), pltpu.VMEM((TM, d), F32)],
        compiler_params=_cparams("parallel", "arbitrary"),
        name="ffn",
    )(x, g_in.reshape(1, d), wg, wu, wd, g_out.reshape(1, d))


def _diff_kernel(qt_ref, k_ref, vt_ref, bias_ref, lam_ref, g_ref, o_ref,
                 qbd_ref, s0_ref, s1_ref, e0_ref, e1_ref, acc_ref, *, lambda_init):
    tq = qt_ref.shape[1]
    nk, _, tk = vt_ref.shape
    i = pl.program_id(2)
    qt = qt_ref[...]
    zero = jnp.zeros((HEAD_DIM, tq), BF16)
    qbd_ref[0:HEAD_DIM, 0:tq] = qt[0:HEAD_DIM]
    qbd_ref[0:HEAD_DIM, tq:] = zero
    qbd_ref[HEAD_DIM:, 0:tq] = zero
    qbd_ref[HEAD_DIM:, tq:] = qt[HEAD_DIM:]
    acc_ref[...] = jnp.zeros_like(acc_ref)
    ones = jnp.ones((BF16_SUBLANES, tk), BF16)

    def scores(j, dst_ref):
        kc = k_ref[pl.ds(pl.multiple_of(j * tk, tk), tk), :]
        dst_ref[...] = jnp.dot(kc, qbd_ref[...], preferred_element_type=F32)

    def softmax(j, src_ref, dst_ref, m_old):
        s = src_ref[...]
        rows = []
        for c in range(tk // TB_DIFF):
            sc = s[c * TB_DIFF:(c + 1) * TB_DIFF]
            cols = []
            for mp in range(2):
                for a in range(tq // TB_DIFF):
                    d = jnp.clip(j * (tk // TB_DIFF) + c - (i * (tq // TB_DIFF) + a), -2, 2) + 2
                    c0 = mp * tq + a * TB_DIFF
                    cols.append(sc[:, c0:c0 + TB_DIFF] + bias_ref[d])
            rows.append(jnp.concatenate(cols, axis=1))
        s = jnp.concatenate(rows, axis=0)
        m_new = jnp.maximum(m_old, jnp.max(s, axis=0, keepdims=True))
        alpha = jnp.exp2(m_old - m_new)
        dst_ref[...] = jnp.exp2(s - m_new).astype(BF16)
        return m_new, alpha

    def accumulate(j, src_ref, alpha):
        vt = jnp.concatenate([vt_ref[j], ones], axis=0)
        acc_ref[...] = alpha * acc_ref[...] + jnp.dot(vt, src_ref[...], preferred_element_type=F32)

    m = jnp.full((1, 2 * tq), NEG, F32)
    scores(0, s0_ref)
    scores(1, s1_ref)
    m, a_prev = softmax(0, s0_ref, e0_ref, m)

    def pair(jj, carry):
        m, a_prev = carry
        j = 2 * jj + 1
        scores(j + 1, s0_ref)
        m, a_odd = softmax(j, s1_ref, e1_ref, m)
        accumulate(j - 1, e0_ref, a_prev)
        scores(j + 2, s1_ref)
        m, a_even = softmax(j + 1, s0_ref, e0_ref, m)
        accumulate(j, e1_ref, a_odd)
        return m, a_even

    m, a_prev = lax.fori_loop(0, (nk - 2) // 2, pair, (m, a_prev))
    m, a_last = softmax(nk - 1, s1_ref, e1_ref, m)
    accumulate(nk - 2, e0_ref, a_prev)
    accumulate(nk - 1, e1_ref, a_last)

    acc = acc_ref[...]
    o = acc[:DIFF_VDIM] / acc[DIFF_VDIM:DIFF_VDIM + 1]
    lp = lam_ref[...]
    lam = (jnp.exp(jnp.sum(lp[0:1] * lp[1:2], axis=-1, keepdims=True))
           - jnp.exp(jnp.sum(lp[2:3] * lp[3:4], axis=-1, keepdims=True)) + lambda_init)
    out = o[:, :tq] - lam * o[:, tq:]
    ms = jnp.mean(out * out, axis=0, keepdims=True)
    y = out * lax.rsqrt(ms + EPS) * g_ref[...] * (1.0 - lambda_init)
    o_ref[...] = y.T.astype(o_ref.dtype)


def _diff_bias(rel_bias):
    t = TB_DIFF
    assert t >= MAX_DISTANCE
    kk = jnp.arange(t)[:, None]
    qq = jnp.arange(t)[None, :]
    rel = jnp.stack([d * t + kk - qq for d in range(-2, 3)])
    return _bias_lookup(rel_bias, rel) * LOG2E


def _diff_attention(qt, k, vt, bias, diff_lambda, subln_g, seq_len, lambda_init):
    n = k.shape[0]
    tq, tk = TQ_DIFF, TK_DIFF
    assert seq_len % tk == 0 and (seq_len // tk) % 2 == 0 and seq_len // tk >= 2
    assert tq % TB_DIFF == 0 and tk % TB_DIFF == 0
    nb = n // seq_len
    nq = seq_len // tq
    nk = seq_len // tk
    return pl.pallas_call(
        functools.partial(_diff_kernel, lambda_init=lambda_init),
        grid=(nb, DIFF_HEADS, nq),
        in_specs=[pl.BlockSpec((DIFF_VDIM, tq), lambda b, h, i: (h, b * nq + i)),
                  pl.BlockSpec((seq_len, DIFF_VDIM), lambda b, h, i: (b, h)),
                  pl.BlockSpec((nk, DIFF_VDIM, tk), lambda b, h, i: (b, h, 0)),
                  pl.BlockSpec((None, 5, TB_DIFF, TB_DIFF), lambda b, h, i: (h, 0, 0, 0)),
                  pl.BlockSpec((4, HEAD_DIM), lambda b, h, i: (0, 0)),
                  pl.BlockSpec((DIFF_VDIM, 1), lambda b, h, i: (0, 0))],
        out_specs=pl.BlockSpec((tq, DIFF_VDIM), lambda b, h, i: (b * nq + i, h)),
        out_shape=jax.ShapeDtypeStruct((n, DIFF_HEADS * DIFF_VDIM), BF16),
        scratch_shapes=[pltpu.VMEM((DIFF_VDIM, 2 * tq), BF16),
                        pltpu.VMEM((tk, 2 * tq), F32),
                        pltpu.VMEM((tk, 2 * tq), F32),
                        pltpu.VMEM((tk, 2 * tq), BF16),
                        pltpu.VMEM((tk, 2 * tq), BF16),
                        pltpu.VMEM((DIFF_VDIM + BF16_SUBLANES, 2 * tq), F32)],
        compiler_params=_cparams("parallel", "parallel", "arbitrary"),
        name="diff_attention",
    )(qt, k, vt, bias, diff_lambda.astype(F32), subln_g.reshape(DIFF_VDIM, 1).astype(F32))


def _moe_kernel(x_ref, gi_ref, rw_ref, rb_ref, wg_ref, wu_ref, wd_ref, go_ref, o_ref,
                h_ref, gate_ref, acc_ref):
    e = pl.program_id(1)
    j = pl.program_id(2)
    col = lax.broadcasted_iota(jnp.int32, gate_ref.shape, 1)

    @pl.when((e == 0) & (j == 0))
    def _():
        h = _rms(x_ref[...], gi_ref[...]).astype(BF16)
        h_ref[...] = h
        acc_ref[...] = jnp.zeros_like(acc_ref)
        logits = jnp.dot(h, rw_ref[...], preferred_element_type=F32) + rb_ref[...]
        m1 = jnp.max(logits, axis=-1, keepdims=True)
        i1 = jnp.min(jnp.where(logits == m1, col, LANES), axis=-1, keepdims=True)
        rest = jnp.where(col == i1, 2 * NEG, logits)
        m2 = jnp.max(rest, axis=-1, keepdims=True)
        i2 = jnp.min(jnp.where(rest == m2, col, LANES), axis=-1, keepdims=True)
        e2 = jnp.exp(m2 - m1)
        den = 1.0 + e2
        gate_ref[...] = jnp.where(col == i1, 1.0 / den, 0.0) + jnp.where(col == i2, e2 / den, 0.0)

    h = h_ref[...]
    a = jnp.dot(h, wg_ref[...], preferred_element_type=F32)
    b = jnp.dot(h, wu_ref[...], preferred_element_type=F32)
    mid = (a * jax.nn.sigmoid(a) * b).astype(BF16)
    y = jnp.dot(mid, wd_ref[...], preferred_element_type=F32)
    gate = jnp.sum(jnp.where(col == e, gate_ref[...], 0.0), axis=-1, keepdims=True)
    acc_ref[...] += gate * y

    @pl.when((e == pl.num_programs(1) - 1) & (j == pl.num_programs(2) - 1))
    def _():
        o_ref[...] = x_ref[...] + _rms(acc_ref[...], go_ref[...])


def _moe(x, g_in, router_w, router_b, wg, wu, wd, g_out):
    n, d = x.shape
    ne, _, dff = wg.shape
    nf = dff // TF
    rw = jnp.zeros((d, LANES), BF16).at[:, :ne].set(router_w.astype(BF16))
    rb = jnp.full((1, LANES), NEG, F32).at[0, :ne].set(router_b.astype(F32))
    return pl.pallas_call(
        _moe_kernel,
        grid=(n // TM, ne, nf),
        in_specs=[pl.BlockSpec((TM, d), lambda i, e, j: (i, 0)),
                  pl.BlockSpec((1, d), lambda i, e, j: (0, 0)),
                  pl.BlockSpec((d, LANES), lambda i, e, j: (0, 0)),
                  pl.BlockSpec((1, LANES), lambda i, e, j: (0, 0)),
                  pl.BlockSpec((None, d, TF), lambda i, e, j: (e, 0, j)),
                  pl.BlockSpec((None, d, TF), lambda i, e, j: (e, 0, j)),
                  pl.BlockSpec((None, TF, d), lambda i, e, j: (e, j, 0)),
                  pl.BlockSpec((1, d), lambda i, e, j: (0, 0))],
        out_specs=pl.BlockSpec((TM, d), lambda i, e, j: (i, 0)),
        out_shape=jax.ShapeDtypeStruct((n, d), F32),
        scratch_shapes=[pltpu.VMEM((TM, d), BF16), pltpu.VMEM((TM, LANES), F32), pltpu.VMEM((TM, d), F32)],
        compiler_params=_cparams("parallel", "arbitrary", "arbitrary"),
        name="moe",
    )(x, g_in.reshape(1, d), rw, rb, wg, wu, wd, g_out.reshape(1, d))


def _trunk(x, seq_len, p):
    n = x.shape[0]
    depth = p["norm_mix"].shape[0]
    for layer in range(depth):
        j = layer // 2
        if layer % 2 == 0:
            o_q = POOL_WIDTH
            o_kv = o_q + WIN_Q_HEADS * HEAD_DIM
            o_end = o_kv + 2 * WIN_KV_HEADS * HEAD_DIM
            u, q, kv = _norm_matmul(x, p["norm_mix"][layer, 0], p["w_in0"][j],
                                    ((0, o_q), (o_q, o_kv), (o_kv, o_end)), (F32, BF16, BF16))
            a = _pool_mixer(u, p["pool_w"][j], p["pool_scale"][j], seq_len)
            b = _window_attention(q, kv, p["win_bias"], p["sink"][j], seq_len)
            w_out = p["w_out0"][j]
            x = _proj_norm_res([a, b], [w_out[:POOL_WIDTH], w_out[POOL_WIDTH:]], x, p["norm_mix"][layer, 1])
            x = _ffn(x, p["norm_ffn"][layer, 0], p["ffn_wg"][j], p["ffn_wu"][j], p["ffn_wd"][j],
                     p["norm_ffn"][layer, 1])
        else:
            lambda_init = 0.8 - 0.6 * math.exp(-0.3 * layer)
            dq = DIFF_HEADS * 2 * HEAD_DIM
            q, k, v = _norm_matmul(x, p["norm_mix"][layer, 0], p["w_in1"][j],
                                   ((0, dq), (dq, 2 * dq), (2 * dq, 3 * dq)), (BF16, BF16, BF16))
            qt = q.T
            vt = v.reshape(n // TK_DIFF, TK_DIFF, dq).transpose(0, 2, 1)
            m = _diff_attention(qt, k, vt, p["diff_bias"], p["diff_lambda"][j], p["subln_g"][j],
                                seq_len, lambda_init)
            x = _proj_norm_res([m], [p["w_out1"][j]], x, p["norm_mix"][layer, 1])
            x = _moe(x, p["norm_ffn"][layer, 0], p["router_w"][j], p["router_b"][j],
                     p["moe_wg"][j], p["moe_wu"][j], p["moe_wd"][j], p["norm_ffn"][layer, 1])
    return x


def _prepare(norm_mix, norm_ffn, rel_bias, w_in0, pool_w, pool_scale, sink, w_out0,
             ffn_wg, ffn_wu, ffn_wd, w_in1, diff_lambda, subln_g, w_out1,
             router_w, router_b, moe_wg, moe_wu, moe_wd):
    scale = HEAD_DIM ** -0.5
    o_q = POOL_WIDTH
    o_kv = o_q + WIN_Q_HEADS * HEAD_DIM
    col0 = jnp.arange(w_in0.shape[-1])
    s0 = jnp.where((col0 >= o_q) & (col0 < o_kv), scale, 1.0).astype(F32)
    col1 = jnp.arange(w_in1.shape[-1])
    s1 = jnp.where(col1 < DIFF_HEADS * 2 * HEAD_DIM, scale * LOG2E, 1.0).astype(F32)
    return dict(
        norm_mix=norm_mix.astype(F32), norm_ffn=norm_ffn.astype(F32),
        win_bias=_window_bias(rel_bias, TQ_WIN), diff_bias=_diff_bias(rel_bias),
        w_in0=(w_in0 * s0).astype(BF16), pool_w=pool_w.astype(BF16), pool_scale=pool_scale.astype(F32),
        sink=sink, w_out0=w_out0.astype(BF16),
        ffn_wg=ffn_wg.astype(BF16), ffn_wu=ffn_wu.astype(BF16), ffn_wd=ffn_wd.astype(BF16),
        w_in1=(w_in1 * s1).astype(BF16), diff_lambda=diff_lambda, subln_g=subln_g,
        w_out1=w_out1.astype(BF16), router_w=router_w, router_b=router_b,
        moe_wg=moe_wg.astype(BF16), moe_wu=moe_wu.astype(BF16), moe_wd=moe_wd.astype(BF16))


def kernel(x_prompt, x_sample, norm_mix, norm_ffn, rel_bias, w_in0, pool_w, pool_scale, sink, w_out0,
           ffn_wg, ffn_wu, ffn_wd, w_in1, diff_lambda, subln_g, w_out1,
           router_w, router_b, moe_wg, moe_wu, moe_wd):
    p = _prepare(norm_mix, norm_ffn, rel_bias, w_in0, pool_w, pool_scale, sink, w_out0,
                 ffn_wg, ffn_wu, ffn_wd, w_in1, diff_lambda, subln_g, w_out1,
                 router_w, router_b, moe_wg, moe_wu, moe_wd)
    outs = []
    for x in (x_prompt, x_sample):
        b, s, d = x.shape
        outs.append(_trunk(x.reshape(b * s, d), s, p).reshape(b, s, d))
    return tuple(outs)
```

```python
import functools
import math

import jax
import jax.numpy as jnp
from jax import lax
from jax.experimental import pallas as pl
from jax.experimental.pallas import tpu as pltpu

D_MODEL = 1024
HEAD_DIM = 64
POOL_WIDTH = 512
POOL_WINDOWS = (2, 4, 8, 16)
POOL_GW = 128
POOL_HALO = 8
WIN_Q_HEADS = 8
WIN_KV_HEADS = 2
WIN_GROUP = 4
WINDOW = 128
DIFF_HEADS = 8
DIFF_VDIM = 128
NUM_BUCKETS = 32
MAX_DISTANCE = 128
D_FF = 2816
N_EXPERTS = 8
EPS = 1e-6
NEG = -1e30
LOG2E = 1.4426950408889634

LANES = 128
BF16_SUBLANES = 16
VMEM_LIMIT = 56 * 1024 * 1024

TM = 512
TF = 1408
TQ_WIN = 256
TQ_DIFF = 256
TK_DIFF = 512
TB_DIFF = 256

F32 = jnp.float32
BF16 = jnp.bfloat16


def _cparams(*sem):
    return pltpu.CompilerParams(dimension_semantics=sem, vmem_limit_bytes=VMEM_LIMIT)


def _rms(xf, g):
    ms = jnp.mean(xf * xf, axis=-1, keepdims=True)
    return xf * lax.rsqrt(ms + EPS) * g


def _t5_bucket(rel):
    half = NUM_BUCKETS // 2
    max_exact = half // 2
    ret = jnp.where(rel > 0, half, 0)
    n = jnp.abs(rel)
    nf = jnp.maximum(n, 1).astype(jnp.float32)
    large = max_exact + (jnp.log(nf / max_exact) / math.log(MAX_DISTANCE / max_exact)
                         * (half - max_exact)).astype(jnp.int32)
    large = jnp.minimum(large, half - 1)
    return ret + jnp.where(n < max_exact, n, large)


def _bias_lookup(rel_bias, rel):
    onehot = (_t5_bucket(rel)[..., None] == jnp.arange(NUM_BUCKETS)).astype(F32)
    out = jnp.einsum("...b,bh->h...", onehot, rel_bias.astype(F32), precision=lax.Precision.HIGHEST)
    return out


def _norm_matmul_kernel(x_ref, g_ref, w_ref, *o_refs, splits):
    h = _rms(x_ref[...], g_ref[...]).astype(BF16)
    z = jnp.dot(h, w_ref[...], preferred_element_type=F32)
    for o_ref, (a, b) in zip(o_refs, splits):
        o_ref[...] = z[:, a:b].astype(o_ref.dtype)


def _norm_matmul(x, g, w, splits, dtypes):
    n, d = x.shape
    nout = w.shape[1]
    return pl.pallas_call(
        functools.partial(_norm_matmul_kernel, splits=splits),
        grid=(n // TM,),
        in_specs=[pl.BlockSpec((TM, d), lambda i: (i, 0)),
                  pl.BlockSpec((1, d), lambda i: (0, 0)),
                  pl.BlockSpec((d, nout), lambda i: (0, 0))],
        out_specs=[pl.BlockSpec((TM, b - a), lambda i: (i, 0)) for a, b in splits],
        out_shape=[jax.ShapeDtypeStruct((n, b - a), dt) for (a, b), dt in zip(splits, dtypes)],
        compiler_params=_cparams("parallel"),
        name="norm_matmul",
    )(x, g.reshape(1, d), w)


def _proj_norm_res_kernel(*refs, n_in):
    ins, ws = refs[:n_in], refs[n_in:2 * n_in]
    x_ref, g_ref, o_ref = refs[2 * n_in:]
    m = jnp.dot(ins[0][...], ws[0][...], preferred_element_type=F32)
    for a_ref, w_ref in zip(ins[1:], ws[1:]):
        m = m + jnp.dot(a_ref[...], w_ref[...], preferred_element_type=F32)
    o_ref[...] = x_ref[...] + _rms(m, g_ref[...])


def _proj_norm_res(ins, ws, x, g):
    n, d = x.shape
    n_in = len(ins)
    return pl.pallas_call(
        functools.partial(_proj_norm_res_kernel, n_in=n_in),
        grid=(n // TM,),
        in_specs=([pl.BlockSpec((TM, a.shape[1]), lambda i: (i, 0)) for a in ins]
                  + [pl.BlockSpec(w.shape, lambda i: (0, 0)) for w in ws]
                  + [pl.BlockSpec((TM, d), lambda i: (i, 0)),
                     pl.BlockSpec((1, d), lambda i: (0, 0))]),
        out_specs=pl.BlockSpec((TM, d), lambda i: (i, 0)),
        out_shape=jax.ShapeDtypeStruct((n, d), F32),
        compiler_params=_cparams("parallel"),
        name="proj_norm_res",
    )(*ins, *ws, x, g.reshape(1, d))


def _pool_kernel(up_ref, u_ref, un_ref, w_ref, sc_ref, o_ref, ext_ref, *, seq_len):
    tm = u_ref.shape[0]
    t0 = (pl.program_id(0) * tm) % seq_len
    ext_ref[0:POOL_HALO, :] = jnp.where(t0 > 0, up_ref[...], 0.0)
    ext_ref[POOL_HALO:POOL_HALO + tm, :] = u_ref[...]
    ext_ref[POOL_HALO + tm:, :] = jnp.where(t0 + tm < seq_len, un_ref[...], 0.0)
    t = t0 + lax.broadcasted_iota(jnp.int32, (tm, 1), 0)
    outs = []
    for g, w in enumerate(POOL_WINDOWS):
        half = w // 2
        sl = slice(g * POOL_GW, (g + 1) * POOL_GW)
        acc = ext_ref[POOL_HALO - half:POOL_HALO - half + tm, sl]
        for j in range(-half + 1, half):
            acc = acc + ext_ref[POOL_HALO + j:POOL_HALO + j + tm, sl]
        lo = jnp.maximum(t - half, 0)
        hi = jnp.minimum(t + half - 1, seq_len - 1)
        cnt = (hi - lo + 1).astype(F32)
        d = (acc / cnt - u_ref[:, sl]).astype(BF16)
        outs.append(jnp.dot(d, w_ref[g], preferred_element_type=F32))
    o_ref[...] = (jnp.concatenate(outs, axis=-1) * sc_ref[...]).astype(o_ref.dtype)


def _pool_mixer(u, pool_w, pool_scale, seq_len):
    n, c = u.shape
    hb = TM // POOL_HALO
    last = n // POOL_HALO - 1
    return pl.pallas_call(
        functools.partial(_pool_kernel, seq_len=seq_len),
        grid=(n // TM,),
        in_specs=[pl.BlockSpec((POOL_HALO, c), lambda i: (jnp.maximum(i * hb - 1, 0), 0)),
                  pl.BlockSpec((TM, c), lambda i: (i, 0)),
                  pl.BlockSpec((POOL_HALO, c), lambda i: (jnp.minimum((i + 1) * hb, last), 0)),
                  pl.BlockSpec(pool_w.shape, lambda i: (0, 0, 0)),
                  pl.BlockSpec((1, c), lambda i: (0, 0))],
        out_specs=pl.BlockSpec((TM, c), lambda i: (i, 0)),
        out_shape=jax.ShapeDtypeStruct((n, c), BF16),
        scratch_shapes=[pltpu.VMEM((TM + 2 * POOL_HALO, c), F32)],
        compiler_params=_cparams("parallel"),
        name="pool_mixer",
    )(u, u, u, pool_w, pool_scale.reshape(1, c))


def _win_kernel(q_ref, kvp_ref, kv_ref, kvn_ref, bias_ref, sink_ref, o_ref, *, seq_len):
    tq = q_ref.shape[0]
    nk = tq + 2 * WINDOW
    t0 = (pl.program_id(0) * tq) % seq_len
    kv = jnp.concatenate([kvp_ref[...], kv_ref[...], kvn_ref[...]], axis=0)
    kpos = t0 - WINDOW + lax.broadcasted_iota(jnp.int32, (1, nk), 1)
    valid = (kpos >= 0) & (kpos < seq_len)
    q = q_ref[...]
    sink = sink_ref[...]
    outs = []
    for g in range(WIN_KV_HEADS):
        k = kv[:, g * HEAD_DIM:(g + 1) * HEAD_DIM]
        v = kv[:, (WIN_KV_HEADS + g) * HEAD_DIM:(WIN_KV_HEADS + g + 1) * HEAD_DIM]
        for r in range(WIN_GROUP):
            h = g * WIN_GROUP + r
            qh = q[:, h * HEAD_DIM:(h + 1) * HEAD_DIM]
            s = lax.dot_general(qh, k, (((1,), (1,)), ((), ())), preferred_element_type=F32)
            s = jnp.where(valid, s + bias_ref[h], NEG)
            sk = sink[:, h:h + 1]
            m = jnp.maximum(jnp.max(s, axis=-1, keepdims=True), sk)
            e = jnp.exp(s - m)
            l = jnp.sum(e, axis=-1, keepdims=True) + jnp.exp(sk - m)
            p = (e / l).astype(BF16)
            outs.append(jnp.dot(p, v, preferred_element_type=F32))
    o_ref[...] = jnp.concatenate(outs, axis=-1).astype(o_ref.dtype)


def _window_bias(rel_bias, tq):
    nk = tq + 2 * WINDOW
    rel = jnp.arange(nk)[None, :] - WINDOW - jnp.arange(tq)[:, None]
    return jnp.where((jnp.abs(rel) <= WINDOW)[None], _bias_lookup(rel_bias, rel), NEG)


def _window_attention(q, kv, bias, sink, seq_len):
    n = q.shape[0]
    tq = TQ_WIN
    hb = tq // WINDOW
    last = n // WINDOW - 1
    return pl.pallas_call(
        functools.partial(_win_kernel, seq_len=seq_len),
        grid=(n // tq,),
        in_specs=[pl.BlockSpec((tq, q.shape[1]), lambda i: (i, 0)),
                  pl.BlockSpec((WINDOW, kv.shape[1]), lambda i: (jnp.maximum(i * hb - 1, 0), 0)),
                  pl.BlockSpec((tq, kv.shape[1]), lambda i: (i, 0)),
                  pl.BlockSpec((WINDOW, kv.shape[1]), lambda i: (jnp.minimum((i + 1) * hb, last), 0)),
                  pl.BlockSpec(bias.shape, lambda i: (0, 0, 0)),
                  pl.BlockSpec((1, WIN_Q_HEADS), lambda i: (0, 0))],
        out_specs=pl.BlockSpec((tq, q.shape[1]), lambda i: (i, 0)),
        out_shape=jax.ShapeDtypeStruct(q.shape, BF16),
        compiler_params=_cparams("parallel"),
        name="window_attention",
    )(q, kv, kv, kv, bias, sink.reshape(1, WIN_Q_HEADS).astype(F32))


def _ffn_kernel(x_ref, gi_ref, wg_ref, wu_ref, wd_ref, go_ref, o_ref, h_ref, acc_ref):
    j = pl.program_id(1)

    @pl.when(j == 0)
    def _():
        h_ref[...] = _rms(x_ref[...], gi_ref[...]).astype(BF16)
        acc_ref[...] = jnp.zeros_like(acc_ref)

    h = h_ref[...]
    a = jnp.dot(h, wg_ref[...], preferred_element_type=F32)
    b = jnp.dot(h, wu_ref[...], preferred_element_type=F32)
    mid = (a * jax.nn.sigmoid(a) * b).astype(BF16)
    acc_ref[...] += jnp.dot(mid, wd_ref[...], preferred_element_type=F32)

    @pl.when(j == pl.num_programs(1) - 1)
    def _():
        o_ref[...] = x_ref[...] + _rms(acc_ref[...], go_ref[...])


def _ffn(x, g_in, wg, wu, wd, g_out):
    n, d = x.shape
    nf = wg.shape[1] // TF
    return pl.pallas_call(
        _ffn_kernel,
        grid=(n // TM, nf),
        in_specs=[pl.BlockSpec((TM, d), lambda i, j: (i, 0)),
                  pl.BlockSpec((1, d), lambda i, j: (0, 0)),
                  pl.BlockSpec((d, TF), lambda i, j: (0, j)),
                  pl.BlockSpec((d, TF), lambda i, j: (0, j)),
                  pl.BlockSpec((TF, d), lambda i, j: (j, 0)),
                  pl.BlockSpec((1, d), lambda i, j: (0, 0))],
        out_specs=pl.BlockSpec((TM, d), lambda i, j: (i, 0)),
        out_shape=jax.ShapeDtypeStruct((n, d), F32),
        scratch_shapes=[pltpu.VMEM((TM, d), BF16), pltpu.VMEM((TM, d), F32)],
        compiler_params=_cparams("parallel", "arbitrary"),
        name="ffn",
    )(x, g_in.reshape(1, d), wg, wu, wd, g_out.reshape(1, d))


def _diff_kernel(qt_ref, k_ref, vt_ref, bias_ref, lam_ref, g_ref, o_ref,
                 qbd_ref, s0_ref, s1_ref, e0_ref, e1_ref, acc_ref, *, lambda_init):
    tq = qt_ref.shape[1]
    nk, _, tk = vt_ref.shape
    i = pl.program_id(2)
    qt = qt_ref[...]
    zero = jnp.zeros((HEAD_DIM, tq), BF16)
    qbd_ref[0:HEAD_DIM, 0:tq] = qt[0:HEAD_DIM]
    qbd_ref[0:HEAD_DIM, tq:] = zero
    qbd_ref[HEAD_DIM:, 0:tq] = zero
    qbd_ref[HEAD_DIM:, tq:] = qt[HEAD_DIM:]
    acc_ref[...] = jnp.zeros_like(acc_ref)
    ones = jnp.ones((BF16_SUBLANES, tk), BF16)

    def scores(j, dst_ref):
        kc = k_ref[pl.ds(pl.multiple_of(j * tk, tk), tk), :]
        s = jnp.dot(kc, qbd_ref[...], preferred_element_type=F32)
        rows = []
        for c in range(tk // TB_DIFF):
            sc = s[c * TB_DIFF:(c + 1) * TB_DIFF]
            cols = []
            for mp in range(2):
                for a in range(tq // TB_DIFF):
                    d = jnp.clip(j * (tk // TB_DIFF) + c - (i * (tq // TB_DIFF) + a), -2, 2) + 2
                    c0 = mp * tq + a * TB_DIFF
                    cols.append(sc[:, c0:c0 + TB_DIFF] + bias_ref[d])
            rows.append(jnp.concatenate(cols, axis=1))
        s = jnp.concatenate(rows, axis=0)
        dst_ref[...] = s
        return jnp.max(s, axis=0, keepdims=True)

    def softmax(src_ref, dst_ref, cmax, m_old):
        m_new = jnp.maximum(m_old, cmax)
        alpha = jnp.exp2(m_old - m_new)
        dst_ref[...] = jnp.exp2(src_ref[...] - m_new).astype(BF16)
        return m_new, alpha

    def accumulate(j, src_ref, alpha):
        vt = jnp.concatenate([vt_ref[j], ones], axis=0)
        acc_ref[...] = alpha * acc_ref[...] + jnp.dot(vt, src_ref[...], preferred_element_type=F32)

    m = jnp.full((1, 2 * tq), NEG, F32)
    c_even = scores(0, s0_ref)
    c_odd = scores(1, s1_ref)
    m, a_prev = softmax(s0_ref, e0_ref, c_even, m)

    def pair(jj, carry):
        m, a_prev, c_odd = carry
        j = 2 * jj + 1
        c_even = scores(j + 1, s0_ref)
        m, a_odd = softmax(s1_ref, e1_ref, c_odd, m)
        accumulate(j - 1, e0_ref, a_prev)
        c_odd = scores(j + 2, s1_ref)
        m, a_even = softmax(s0_ref, e0_ref, c_even, m)
        accumulate(j, e1_ref, a_odd)
        return m, a_even, c_odd

    m, a_prev, c_odd = lax.fori_loop(0, (nk - 2) // 2, pair, (m, a_prev, c_odd))
    m, a_last = softmax(s1_ref, e1_ref, c_odd, m)
    accumulate(nk - 2, e0_ref, a_prev)
    accumulate(nk - 1, e1_ref, a_last)

    acc = acc_ref[...]
    o = acc[:DIFF_VDIM] / acc[DIFF_VDIM:DIFF_VDIM + 1]
    lp = lam_ref[...]
    lam = (jnp.exp(jnp.sum(lp[0:1] * lp[1:2], axis=-1, keepdims=True))
           - jnp.exp(jnp.sum(lp[2:3] * lp[3:4], axis=-1, keepdims=True)) + lambda_init)
    out = o[:, :tq] - lam * o[:, tq:]
    ms = jnp.mean(out * out, axis=0, keepdims=True)
    y = out * lax.rsqrt(ms + EPS) * g_ref[...] * (1.0 - lambda_init)
    o_ref[...] = y.T.astype(o_ref.dtype)


def _diff_bias(rel_bias):
    t = TB_DIFF
    assert t >= MAX_DISTANCE
    kk = jnp.arange(t)[:, None]
    qq = jnp.arange(t)[None, :]
    rel = jnp.stack([d * t + kk - qq for d in range(-2, 3)])
    return _bias_lookup(rel_bias, rel) * LOG2E


def _diff_attention(qt, k, vt, bias, diff_lambda, subln_g, seq_len, lambda_init):
    n = k.shape[0]
    tq, tk = TQ_DIFF, TK_DIFF
    assert seq_len % tk == 0 and (seq_len // tk) % 2 == 0 and seq_len // tk >= 2
    assert tq % TB_DIFF == 0 and tk % TB_DIFF == 0
    nb = n // seq_len
    nq = seq_len // tq
    nk = seq_len // tk
    return pl.pallas_call(
        functools.partial(_diff_kernel, lambda_init=lambda_init),
        grid=(nb, DIFF_HEADS, nq),
        in_specs=[pl.BlockSpec((DIFF_VDIM, tq), lambda b, h, i: (h, b * nq + i)),
                  pl.BlockSpec((seq_len, DIFF_VDIM), lambda b, h, i: (b, h)),
                  pl.BlockSpec((nk, DIFF_VDIM, tk), lambda b, h, i: (b, h, 0)),
                  pl.BlockSpec((None, 5, TB_DIFF, TB_DIFF), lambda b, h, i: (h, 0, 0, 0)),
                  pl.BlockSpec((4, HEAD_DIM), lambda b, h, i: (0, 0)),
                  pl.BlockSpec((DIFF_VDIM, 1), lambda b, h, i: (0, 0))],
        out_specs=pl.BlockSpec((tq, DIFF_VDIM), lambda b, h, i: (b * nq + i, h)),
        out_shape=jax.ShapeDtypeStruct((n, DIFF_HEADS * DIFF_VDIM), BF16),
        scratch_shapes=[pltpu.VMEM((DIFF_VDIM, 2 * tq), BF16),
                        pltpu.VMEM((tk, 2 * tq), F32),
                        pltpu.VMEM((tk, 2 * tq), F32),
                        pltpu.VMEM((tk, 2 * tq), BF16),
                        pltpu.VMEM((tk, 2 * tq), BF16),
                        pltpu.VMEM((DIFF_VDIM + BF16_SUBLANES, 2 * tq), F32)],
        compiler_params=_cparams("parallel", "parallel", "arbitrary"),
        name="diff_attention",
    )(qt, k, vt, bias, diff_lambda.astype(F32), subln_g.reshape(DIFF_VDIM, 1).astype(F32))


def _router_kernel(x_ref, gi_ref, rw_ref, rb_ref, ti_ref, tw_ref):
    h = _rms(x_ref[...], gi_ref[...]).astype(BF16)
    logits = jnp.dot(h, rw_ref[...], preferred_element_type=F32) + rb_ref[...]
    col = lax.broadcasted_iota(jnp.int32, logits.shape, 1)
    m1 = jnp.max(logits, axis=-1, keepdims=True)
    i1 = jnp.min(jnp.where(logits == m1, col, LANES), axis=-1, keepdims=True)
    rest = jnp.where(col == i1, 2 * NEG, logits)
    m2 = jnp.max(rest, axis=-1, keepdims=True)
    i2 = jnp.min(jnp.where(rest == m2, col, LANES), axis=-1, keepdims=True)
    e2 = jnp.exp(m2 - m1)
    den = 1.0 + e2
    ti_ref[...] = jnp.concatenate([i1, i2], axis=1)
    tw_ref[...] = jnp.concatenate([1.0 / den, e2 / den], axis=1)


def _router(x, g_in, router_w, router_b):
    n, d = x.shape
    ne = router_w.shape[1]
    rw = jnp.zeros((d, LANES), BF16).at[:, :ne].set(router_w.astype(BF16))
    rb = jnp.full((1, LANES), NEG, F32).at[0, :ne].set(router_b.astype(F32))
    return pl.pallas_call(
        _router_kernel,
        grid=(n // TM,),
        in_specs=[pl.BlockSpec((TM, d), lambda i: (i, 0)),
                  pl.BlockSpec((1, d), lambda i: (0, 0)),
                  pl.BlockSpec((d, LANES), lambda i: (0, 0)),
                  pl.BlockSpec((1, LANES), lambda i: (0, 0))],
        out_specs=[pl.BlockSpec((TM, 2), lambda i: (i, 0)), pl.BlockSpec((TM, 2), lambda i: (i, 0))],
        out_shape=[jax.ShapeDtypeStruct((n, 2), jnp.int32), jax.ShapeDtypeStruct((n, 2), F32)],
        compiler_params=_cparams("parallel"),
        name="moe_router",
    )(x, g_in.reshape(1, d), rw, rb)


def _route(topi, topw, ne, t):
    n = topi.shape[0]
    p = 2 * n
    e = topi.reshape(p)
    onehot = (e[:, None] == jnp.arange(ne)[None, :]).astype(jnp.int32)
    csum = jnp.cumsum(onehot, axis=0)
    rank = jnp.sum(onehot * csum, axis=1) - 1
    padded = ((csum[-1] + t - 1) // t) * t
    ends = jnp.cumsum(padded)
    dest = jnp.sum(onehot * (ends - padded)[None, :], axis=1) + rank
    n_tiles = p // t + ne
    src = jnp.zeros((n_tiles * t,), jnp.int32).at[dest].set(jnp.arange(p, dtype=jnp.int32) // 2)
    gate = jnp.zeros((n_tiles * t,), F32).at[dest].set(topw.reshape(p))
    tile_expert = jnp.sum(jnp.arange(n_tiles, dtype=jnp.int32)[:, None] * t >= ends[None, :], axis=1)
    tile_expert = jnp.minimum(tile_expert, ne - 1).astype(jnp.int32)
    dest_tiles = dest.reshape(n // TM, TM, 2).transpose(0, 2, 1).reshape(n // TM, 2 * TM)
    return src.reshape(n_tiles, t), gate.reshape(n_tiles * t, 1), tile_expert, dest_tiles.astype(jnp.int32)


def _row_copy(src_hbm, row, dst_ref, r, sem):
    return pltpu.make_async_copy(src_hbm.at[pl.ds(row, 1)], dst_ref.at[pl.ds(r, 1)], sem)


def _start_row_gather(idx_ref, islot, base, n_rows, src_hbm, dst_ref, sem):
    def body(r, c):
        _row_copy(src_hbm, idx_ref[islot, base + r], dst_ref, r, sem).start()
        return c
    lax.fori_loop(0, n_rows, body, 0, unroll=8)


def _wait_row_gather(n_rows, src_hbm, dst_ref, sem):
    def body(r, c):
        _row_copy(src_hbm, 0, dst_ref, 0, sem).wait()
        return c
    lax.fori_loop(0, n_rows, body, 0, unroll=8)


def _prefetch_schedule(t, n_steps, idx_hbm, idx_ref, isem, start_gather):
    def idx_copy(step):
        return pltpu.make_async_copy(idx_hbm.at[step], idx_ref.at[step % 3], isem.at[step % 3])

    @pl.when(t == 0)
    def _():
        idx_copy(0).start()
        idx_copy(0).wait()
        start_gather(0)
        if n_steps > 1:
            idx_copy(1).start()

    @pl.when(t + 1 < n_steps)
    def _():
        idx_copy(t + 1).wait()

        @pl.when(t + 2 < n_steps)
        def _():
            idx_copy(t + 2).start()

        start_gather(t + 1)


def _moe_expert_kernel(te_ref, src_hbm, gate_ref, x_hbm, gi_ref, wg_ref, wu_ref, wd_ref, o_ref,
                       idx_ref, xg_ref, isem, gsem, h_ref, acc_ref):
    t = pl.program_id(0)
    j = pl.program_id(1)
    rows = xg_ref.shape[1]

    @pl.when(j == 0)
    def _():
        def start_gather(step):
            _start_row_gather(idx_ref, step % 3, 0, rows, x_hbm, xg_ref.at[step % 2], gsem.at[step % 2])

        _prefetch_schedule(t, src_hbm.shape[0], src_hbm, idx_ref, isem, start_gather)
        _wait_row_gather(rows, x_hbm, xg_ref.at[t % 2], gsem.at[t % 2])
        h_ref[...] = _rms(xg_ref[t % 2], gi_ref[...]).astype(BF16)
        acc_ref[...] = jnp.zeros_like(acc_ref)

    h = h_ref[...]
    a = jnp.dot(h, wg_ref[...], preferred_element_type=F32)
    b = jnp.dot(h, wu_ref[...], preferred_element_type=F32)
    mid = (a * jax.nn.sigmoid(a) * b).astype(BF16)
    acc_ref[...] += jnp.dot(mid, wd_ref[...], preferred_element_type=F32)

    @pl.when(j == pl.num_programs(1) - 1)
    def _():
        o_ref[...] = gate_ref[...] * acc_ref[...]


def _moe_experts(x, g_in, src, gate, tile_expert, wg, wu, wd):
    n, d = x.shape
    n_tiles, t = src.shape
    nf = wg.shape[2] // TF
    return pl.pallas_call(
        _moe_expert_kernel,
        grid_spec=pltpu.PrefetchScalarGridSpec(
            num_scalar_prefetch=1,
            grid=(n_tiles, nf),
            in_specs=[pl.BlockSpec(memory_space=pl.ANY),
                      pl.BlockSpec((t, 1), lambda i, j, te: (i, 0)),
                      pl.BlockSpec(memory_space=pl.ANY),
                      pl.BlockSpec((1, d), lambda i, j, te: (0, 0)),
                      pl.BlockSpec((None, d, TF), lambda i, j, te: (te[i], 0, j)),
                      pl.BlockSpec((None, d, TF), lambda i, j, te: (te[i], 0, j)),
                      pl.BlockSpec((None, TF, d), lambda i, j, te: (te[i], j, 0))],
            out_specs=pl.BlockSpec((t, d), lambda i, j, te: (i, 0)),
            scratch_shapes=[pltpu.SMEM((3, t), jnp.int32),
                            pltpu.VMEM((2, t, d), F32),
                            pltpu.SemaphoreType.DMA((3,)),
                            pltpu.SemaphoreType.DMA((2,)),
                            pltpu.VMEM((t, d), BF16),
                            pltpu.VMEM((t, d), F32)]),
        out_shape=jax.ShapeDtypeStruct((n_tiles * t, d), F32),
        compiler_params=_cparams("arbitrary", "arbitrary"),
        name="moe_experts",
    )(tile_expert, src, gate, x, g_in.reshape(1, d), wg, wu, wd)


def _moe_combine_kernel(dest_hbm, y_hbm, x_ref, go_ref, o_ref, idx_ref, buf_ref, isem, gsem):
    i = pl.program_id(0)
    tm = x_ref.shape[0]

    def start_gather(step):
        for slot in range(2):
            _start_row_gather(idx_ref, step % 3, slot * tm, tm, y_hbm,
                              buf_ref.at[step % 2, slot], gsem.at[step % 2])

    _prefetch_schedule(i, dest_hbm.shape[0], dest_hbm, idx_ref, isem, start_gather)
    for slot in range(2):
        _wait_row_gather(tm, y_hbm, buf_ref.at[i % 2, slot], gsem.at[i % 2])
    y = buf_ref[i % 2, 0] + buf_ref[i % 2, 1]
    o_ref[...] = x_ref[...] + _rms(y, go_ref[...])


def _moe_combine(y, dest_tiles, x, g_out):
    n, d = x.shape
    return pl.pallas_call(
        _moe_combine_kernel,
        grid=(n // TM,),
        in_specs=[pl.BlockSpec(memory_space=pl.ANY),
                  pl.BlockSpec(memory_space=pl.ANY),
                  pl.BlockSpec((TM, d), lambda i: (i, 0)),
                  pl.BlockSpec((1, d), lambda i: (0, 0))],
        out_specs=pl.BlockSpec((TM, d), lambda i: (i, 0)),
        out_shape=jax.ShapeDtypeStruct((n, d), F32),
        scratch_shapes=[pltpu.SMEM((3, 2 * TM), jnp.int32),
                        pltpu.VMEM((2, 2, TM, d), F32),
                        pltpu.SemaphoreType.DMA((3,)),
                        pltpu.SemaphoreType.DMA((2,))],
        compiler_params=_cparams("arbitrary"),
        name="moe_combine",
    )(dest_tiles, y, x, g_out.reshape(1, d))


def _moe(x, g_in, router_w, router_b, wg, wu, wd, g_out):
    topi, topw = _router(x, g_in, router_w, router_b)
    src, gate, tile_expert, dest_tiles = _route(topi, topw, wg.shape[0], TM)
    y = _moe_experts(x, g_in, src, gate, tile_expert, wg, wu, wd)
    return _moe_combine(y, dest_tiles, x, g_out)


def _trunk(x, seq_len, p):
    n = x.shape[0]
    depth = p["norm_mix"].shape[0]
    for layer in range(depth):
        j = layer // 2
        if layer % 2 == 0:
            o_q = POOL_WIDTH
            o_kv = o_q + WIN_Q_HEADS * HEAD_DIM
            o_end = o_kv + 2 * WIN_KV_HEADS * HEAD_DIM
            u, q, kv = _norm_matmul(x, p["norm_mix"][layer, 0], p["w_in0"][j],
                                    ((0, o_q), (o_q, o_kv), (o_kv, o_end)), (F32, BF16, BF16))
            a = _pool_mixer(u, p["pool_w"][j], p["pool_scale"][j], seq_len)
            b = _window_attention(q, kv, p["win_bias"], p["sink"][j], seq_len)
            w_out = p["w_out0"][j]
            x = _proj_norm_res([a, b], [w_out[:POOL_WIDTH], w_out[POOL_WIDTH:]], x, p["norm_mix"][layer, 1])
            x = _ffn(x, p["norm_ffn"][layer, 0], p["ffn_wg"][j], p["ffn_wu"][j], p["ffn_wd"][j],
                     p["norm_ffn"][layer, 1])
        else:
            lambda_init = 0.8 - 0.6 * math.exp(-0.3 * layer)
            dq = DIFF_HEADS * 2 * HEAD_DIM
            q, k, v = _norm_matmul(x, p["norm_mix"][layer, 0], p["w_in1"][j],
                                   ((0, dq), (dq, 2 * dq), (2 * dq, 3 * dq)), (BF16, BF16, BF16))
            qt = q.T
            vt = v.reshape(n // TK_DIFF, TK_DIFF, dq).transpose(0, 2, 1)
            m = _diff_attention(qt, k, vt, p["diff_bias"], p["diff_lambda"][j], p["subln_g"][j],
                                seq_len, lambda_init)
            x = _proj_norm_res([m], [p["w_out1"][j]], x, p["norm_mix"][layer, 1])
            x = _moe(x, p["norm_ffn"][layer, 0], p["router_w"][j], p["router_b"][j],
                     p["moe_wg"][j], p["moe_wu"][j], p["moe_wd"][j], p["norm_ffn"][layer, 1])
    return x


def _prepare(norm_mix, norm_ffn, rel_bias, w_in0, pool_w, pool_scale, sink, w_out0,
             ffn_wg, ffn_wu, ffn_wd, w_in1, diff_lambda, subln_g, w_out1,
             router_w, router_b, moe_wg, moe_wu, moe_wd):
    scale = HEAD_DIM ** -0.5
    o_q = POOL_WIDTH
    o_kv = o_q + WIN_Q_HEADS * HEAD_DIM
    col0 = jnp.arange(w_in0.shape[-1])
    s0 = jnp.where((col0 >= o_q) & (col0 < o_kv), scale, 1.0).astype(F32)
    col1 = jnp.arange(w_in1.shape[-1])
    s1 = jnp.where(col1 < DIFF_HEADS * 2 * HEAD_DIM, scale * LOG2E, 1.0).astype(F32)
    return dict(
        norm_mix=norm_mix.astype(F32), norm_ffn=norm_ffn.astype(F32),
        win_bias=_window_bias(rel_bias, TQ_WIN), diff_bias=_diff_bias(rel_bias),
        w_in0=(w_in0 * s0).astype(BF16), pool_w=pool_w.astype(BF16), pool_scale=pool_scale.astype(F32),
        sink=sink, w_out0=w_out0.astype(BF16),
        ffn_wg=ffn_wg.astype(BF16), ffn_wu=ffn_wu.astype(BF16), ffn_wd=ffn_wd.astype(BF16),
        w_in1=(w_in1 * s1).astype(BF16), diff_lambda=diff_lambda, subln_g=subln_g,
        w_out1=w_out1.astype(BF16), router_w=router_w, router_b=router_b,
        moe_wg=moe_wg.astype(BF16), moe_wu=moe_wu.astype(BF16), moe_wd=moe_wd.astype(BF16))


def kernel(x_prompt, x_sample, norm_mix, norm_ffn, rel_bias, w_in0, pool_w, pool_scale, sink, w_out0,
           ffn_wg, ffn_wu, ffn_wd, w_in1, diff_lambda, subln_g, w_out1,
           router_w, router_b, moe_wg, moe_wu, moe_wd):
    p = _prepare(norm_mix, norm_ffn, rel_bias, w_in0, pool_w, pool_scale, sink, w_out0,
                 ffn_wg, ffn_wu, ffn_wd, w_in1, diff_lambda, subln_g, w_out1,
                 router_w, router_b, moe_wg, moe_wu, moe_wd)
    outs = []
    for x in (x_prompt, x_sample):
        b, s, d = x.shape
        outs.append(_trunk(x.reshape(b * s, d), s, p).reshape(b, s, d))
    return tuple(outs)
```

```python
import functools
import math

import jax
import jax.numpy as jnp
from jax import lax
from jax.experimental import pallas as pl
from jax.experimental.pallas import tpu as pltpu

D_MODEL = 1024
HEAD_DIM = 64
POOL_WIDTH = 512
POOL_WINDOWS = (2, 4, 8, 16)
POOL_GW = 128
POOL_HALO = 8
WIN_Q_HEADS = 8
WIN_KV_HEADS = 2
WIN_GROUP = 4
WINDOW = 128
DIFF_HEADS = 8
DIFF_VDIM = 128
NUM_BUCKETS = 32
MAX_DISTANCE = 128
D_FF = 2816
N_EXPERTS = 8
EPS = 1e-6
NEG = -1e30
LOG2E = 1.4426950408889634

LANES = 128
BF16_SUBLANES = 16
VMEM_LIMIT = 56 * 1024 * 1024

TM = 512
TF = 1408
TQ_WIN = 256
TQ_DIFF = 256
TK_DIFF = 512
TB_DIFF = 256
DIFF_UNROLL = 4

F32 = jnp.float32
BF16 = jnp.bfloat16


def _cparams(*sem):
    return pltpu.CompilerParams(dimension_semantics=sem, vmem_limit_bytes=VMEM_LIMIT)


def _rms(xf, g):
    ms = jnp.mean(xf * xf, axis=-1, keepdims=True)
    return xf * lax.rsqrt(ms + EPS) * g


def _t5_bucket(rel):
    half = NUM_BUCKETS // 2
    max_exact = half // 2
    ret = jnp.where(rel > 0, half, 0)
    n = jnp.abs(rel)
    nf = jnp.maximum(n, 1).astype(jnp.float32)
    large = max_exact + (jnp.log(nf / max_exact) / math.log(MAX_DISTANCE / max_exact)
                         * (half - max_exact)).astype(jnp.int32)
    large = jnp.minimum(large, half - 1)
    return ret + jnp.where(n < max_exact, n, large)


def _bias_lookup(rel_bias, rel):
    onehot = (_t5_bucket(rel)[..., None] == jnp.arange(NUM_BUCKETS)).astype(F32)
    out = jnp.einsum("...b,bh->h...", onehot, rel_bias.astype(F32), precision=lax.Precision.HIGHEST)
    return out


def _norm_matmul_kernel(x_ref, g_ref, w_ref, *o_refs, splits):
    h = _rms(x_ref[...], g_ref[...]).astype(BF16)
    z = jnp.dot(h, w_ref[...], preferred_element_type=F32)
    for o_ref, (a, b) in zip(o_refs, splits):
        o_ref[...] = z[:, a:b].astype(o_ref.dtype)


def _norm_matmul(x, g, w, splits, dtypes):
    n, d = x.shape
    nout = w.shape[1]
    return pl.pallas_call(
        functools.partial(_norm_matmul_kernel, splits=splits),
        grid=(n // TM,),
        in_specs=[pl.BlockSpec((TM, d), lambda i: (i, 0)),
                  pl.BlockSpec((1, d), lambda i: (0, 0)),
                  pl.BlockSpec((d, nout), lambda i: (0, 0))],
        out_specs=[pl.BlockSpec((TM, b - a), lambda i: (i, 0)) for a, b in splits],
        out_shape=[jax.ShapeDtypeStruct((n, b - a), dt) for (a, b), dt in zip(splits, dtypes)],
        compiler_params=_cparams("parallel"),
        name="norm_matmul",
    )(x, g.reshape(1, d), w)


def _proj_norm_res_kernel(*refs, n_in):
    ins, ws = refs[:n_in], refs[n_in:2 * n_in]
    x_ref, g_ref, o_ref = refs[2 * n_in:]
    m = jnp.dot(ins[0][...], ws[0][...], preferred_element_type=F32)
    for a_ref, w_ref in zip(ins[1:], ws[1:]):
        m = m + jnp.dot(a_ref[...], w_ref[...], preferred_element_type=F32)
    o_ref[...] = x_ref[...] + _rms(m, g_ref[...])


def _proj_norm_res(ins, ws, x, g):
    n, d = x.shape
    n_in = len(ins)
    return pl.pallas_call(
        functools.partial(_proj_norm_res_kernel, n_in=n_in),
        grid=(n // TM,),
        in_specs=([pl.BlockSpec((TM, a.shape[1]), lambda i: (i, 0)) for a in ins]
                  + [pl.BlockSpec(w.shape, lambda i: (0, 0)) for w in ws]
                  + [pl.BlockSpec((TM, d), lambda i: (i, 0)),
                     pl.BlockSpec((1, d), lambda i: (0, 0))]),
        out_specs=pl.BlockSpec((TM, d), lambda i: (i, 0)),
        out_shape=jax.ShapeDtypeStruct((n, d), F32),
        compiler_params=_cparams("parallel"),
        name="proj_norm_res",
    )(*ins, *ws, x, g.reshape(1, d))


def _pool_kernel(up_ref, u_ref, un_ref, w_ref, sc_ref, o_ref, ext_ref, *, seq_len):
    tm = u_ref.shape[0]
    t0 = (pl.program_id(0) * tm) % seq_len
    ext_ref[0:POOL_HALO, :] = jnp.where(t0 > 0, up_ref[...], 0.0)
    ext_ref[POOL_HALO:POOL_HALO + tm, :] = u_ref[...]
    ext_ref[POOL_HALO + tm:, :] = jnp.where(t0 + tm < seq_len, un_ref[...], 0.0)
    t = t0 + lax.broadcasted_iota(jnp.int32, (tm, 1), 0)
    outs = []
    for g, w in enumerate(POOL_WINDOWS):
        half = w // 2
        sl = slice(g * POOL_GW, (g + 1) * POOL_GW)
        acc = ext_ref[POOL_HALO - half:POOL_HALO - half + tm, sl]
        for j in range(-half + 1, half):
            acc = acc + ext_ref[POOL_HALO + j:POOL_HALO + j + tm, sl]
        lo = jnp.maximum(t - half, 0)
        hi = jnp.minimum(t + half - 1, seq_len - 1)
        cnt = (hi - lo + 1).astype(F32)
        d = (acc / cnt - u_ref[:, sl]).astype(BF16)
        outs.append(jnp.dot(d, w_ref[g], preferred_element_type=F32))
    o_ref[...] = (jnp.concatenate(outs, axis=-1) * sc_ref[...]).astype(o_ref.dtype)


def _pool_mixer(u, pool_w, pool_scale, seq_len):
    n, c = u.shape
    hb = TM // POOL_HALO
    last = n // POOL_HALO - 1
    return pl.pallas_call(
        functools.partial(_pool_kernel, seq_len=seq_len),
        grid=(n // TM,),
        in_specs=[pl.BlockSpec((POOL_HALO, c), lambda i: (jnp.maximum(i * hb - 1, 0), 0)),
                  pl.BlockSpec((TM, c), lambda i: (i, 0)),
                  pl.BlockSpec((POOL_HALO, c), lambda i: (jnp.minimum((i + 1) * hb, last), 0)),
                  pl.BlockSpec(pool_w.shape, lambda i: (0, 0, 0)),
                  pl.BlockSpec((1, c), lambda i: (0, 0))],
        out_specs=pl.BlockSpec((TM, c), lambda i: (i, 0)),
        out_shape=jax.ShapeDtypeStruct((n, c), BF16),
        scratch_shapes=[pltpu.VMEM((TM + 2 * POOL_HALO, c), F32)],
        compiler_params=_cparams("parallel"),
        name="pool_mixer",
    )(u, u, u, pool_w, pool_scale.reshape(1, c))


def _win_kernel(q_ref, kvp_ref, kv_ref, kvn_ref, bias_ref, sink_ref, o_ref, *, seq_len):
    tq = q_ref.shape[0]
    nk = tq + 2 * WINDOW
    t0 = (pl.program_id(0) * tq) % seq_len
    kv = jnp.concatenate([kvp_ref[...], kv_ref[...], kvn_ref[...]], axis=0)
    kpos = t0 - WINDOW + lax.broadcasted_iota(jnp.int32, (1, nk), 1)
    valid = (kpos >= 0) & (kpos < seq_len)
    q = q_ref[...]
    sink = sink_ref[...]
    outs = []
    for g in range(WIN_KV_HEADS):
        k = kv[:, g * HEAD_DIM:(g + 1) * HEAD_DIM]
        v = kv[:, (WIN_KV_HEADS + g) * HEAD_DIM:(WIN_KV_HEADS + g + 1) * HEAD_DIM]
        for r in range(WIN_GROUP):
            h = g * WIN_GROUP + r
            qh = q[:, h * HEAD_DIM:(h + 1) * HEAD_DIM]
            s = lax.dot_general(qh, k, (((1,), (1,)), ((), ())), preferred_element_type=F32)
            s = jnp.where(valid, s + bias_ref[h], NEG)
            sk = sink[:, h:h + 1]
            m = jnp.maximum(jnp.max(s, axis=-1, keepdims=True), sk)
            e = jnp.exp(s - m)
            l = jnp.sum(e, axis=-1, keepdims=True) + jnp.exp(sk - m)
            p = (e / l).astype(BF16)
            outs.append(jnp.dot(p, v, preferred_element_type=F32))
    o_ref[...] = jnp.concatenate(outs, axis=-1).astype(o_ref.dtype)


def _window_bias(rel_bias, tq):
    nk = tq + 2 * WINDOW
    rel = jnp.arange(nk)[None, :] - WINDOW - jnp.arange(tq)[:, None]
    return jnp.where((jnp.abs(rel) <= WINDOW)[None], _bias_lookup(rel_bias, rel), NEG)


def _window_attention(q, kv, bias, sink, seq_len):
    n = q.shape[0]
    tq = TQ_WIN
    hb = tq // WINDOW
    last = n // WINDOW - 1
    return pl.pallas_call(
        functools.partial(_win_kernel, seq_len=seq_len),
        grid=(n // tq,),
        in_specs=[pl.BlockSpec((tq, q.shape[1]), lambda i: (i, 0)),
                  pl.BlockSpec((WINDOW, kv.shape[1]), lambda i: (jnp.maximum(i * hb - 1, 0), 0)),
                  pl.BlockSpec((tq, kv.shape[1]), lambda i: (i, 0)),
                  pl.BlockSpec((WINDOW, kv.shape[1]), lambda i: (jnp.minimum((i + 1) * hb, last), 0)),
                  pl.BlockSpec(bias.shape, lambda i: (0, 0, 0)),
                  pl.BlockSpec((1, WIN_Q_HEADS), lambda i: (0, 0))],
        out_specs=pl.BlockSpec((tq, q.shape[1]), lambda i: (i, 0)),
        out_shape=jax.ShapeDtypeStruct(q.shape, BF16),
        compiler_params=_cparams("parallel"),
        name="window_attention",
    )(q, kv, kv, kv, bias, sink.reshape(1, WIN_Q_HEADS).astype(F32))


def _ffn_kernel(x_ref, gi_ref, wg_ref, wu_ref, wd_ref, go_ref, o_ref, h_ref, acc_ref):
    j = pl.program_id(1)

    @pl.when(j == 0)
    def _():
        h_ref[...] = _rms(x_ref[...], gi_ref[...]).astype(BF16)
        acc_ref[...] = jnp.zeros_like(acc_ref)

    h = h_ref[...]
    a = jnp.dot(h, wg_ref[...], preferred_element_type=F32)
    b = jnp.dot(h, wu_ref[...], preferred_element_type=F32)
    mid = (a * jax.nn.sigmoid(a) * b).astype(BF16)
    acc_ref[...] += jnp.dot(mid, wd_ref[...], preferred_element_type=F32)

    @pl.when(j == pl.num_programs(1) - 1)
    def _():
        o_ref[...] = x_ref[...] + _rms(acc_ref[...], go_ref[...])


def _ffn(x, g_in, wg, wu, wd, g_out):
    n, d = x.shape
    nf = wg.shape[1] // TF
    return pl.pallas_call(
        _ffn_kernel,
        grid=(n // TM, nf),
        in_specs=[pl.BlockSpec((TM, d), lambda i, j: (i, 0)),
                  pl.BlockSpec((1, d), lambda i, j: (0, 0)),
                  pl.BlockSpec((d, TF), lambda i, j: (0, j)),
                  pl.BlockSpec((d, TF), lambda i, j: (0, j)),
                  pl.BlockSpec((TF, d), lambda i, j: (j, 0)),
                  pl.BlockSpec((1, d), lambda i, j: (0, 0))],
        out_specs=pl.BlockSpec((TM, d), lambda i, j: (i, 0)),
        out_shape=jax.ShapeDtypeStruct((n, d), F32),
        scratch_shapes=[pltpu.VMEM((TM, d), BF16), pltpu.VMEM((TM, d), F32)],
        compiler_params=_cparams("parallel", "arbitrary"),
        name="ffn",
    )(x, g_in.reshape(1, d), wg, wu, wd, g_out.reshape(1, d))


def _diff_kernel(qt_ref, k_ref, vt_ref, bias_ref, lam_ref, g_ref, o_ref,
                 qbd_ref, s0_ref, s1_ref, e0_ref, e1_ref, acc_ref, *, lambda_init):
    tq = qt_ref.shape[1]
    nk, _, tk = vt_ref.shape
    i = pl.program_id(2)
    qt = qt_ref[...]
    zero = jnp.zeros((HEAD_DIM, tq), BF16)
    qbd_ref[0:HEAD_DIM, 0:tq] = qt[0:HEAD_DIM]
    qbd_ref[0:HEAD_DIM, tq:] = zero
    qbd_ref[HEAD_DIM:, 0:tq] = zero
    qbd_ref[HEAD_DIM:, tq:] = qt[HEAD_DIM:]
    acc_ref[...] = jnp.zeros_like(acc_ref)
    ones = jnp.ones((BF16_SUBLANES, tk), BF16)

    def scores(j, dst_ref):
        kc = k_ref[pl.ds(pl.multiple_of(j * tk, tk), tk), :]
        s = jnp.dot(kc, qbd_ref[...], preferred_element_type=F32).astype(BF16)
        rows = []
        for c in range(tk // TB_DIFF):
            sc = s[c * TB_DIFF:(c + 1) * TB_DIFF]
            cols = []
            for mp in range(2):
                for a in range(tq // TB_DIFF):
                    d = jnp.clip(j * (tk // TB_DIFF) + c - (i * (tq // TB_DIFF) + a), -2, 2) + 2
                    c0 = mp * tq + a * TB_DIFF
                    cols.append(sc[:, c0:c0 + TB_DIFF] + bias_ref[d])
            rows.append(jnp.concatenate(cols, axis=1))
        s = jnp.concatenate(rows, axis=0)
        dst_ref[...] = s
        return jnp.max(s, axis=0, keepdims=True)

    def softmax(src_ref, dst_ref, cmax, m_old):
        m_new = jnp.maximum(m_old, cmax.astype(F32))
        alpha = jnp.exp2(m_old - m_new)
        dst_ref[...] = jnp.exp2(src_ref[...] - m_new.astype(BF16))
        return m_new, alpha

    def accumulate(j, src_ref, alpha):
        vt = jnp.concatenate([vt_ref[j], ones], axis=0)
        acc_ref[...] = alpha * acc_ref[...] + jnp.dot(vt, src_ref[...], preferred_element_type=F32)

    s_refs = (s0_ref, s1_ref)
    e_refs = (e0_ref, e1_ref)

    def step(j, parity, carry):
        m, a_prev, c_cur = carry
        c_next = scores(j + 1, s_refs[1 - parity])
        m, a_cur = softmax(s_refs[parity], e_refs[parity], c_cur, m)
        accumulate(j - 1, e_refs[1 - parity], a_prev)
        return m, a_cur, c_next

    m = jnp.full((1, 2 * tq), NEG, F32)
    c0 = scores(0, s0_ref)
    c1 = scores(1, s1_ref)
    m, a0 = softmax(s0_ref, e0_ref, c0, m)
    carry = (m, a0, c1)

    n_trips = (nk - 2) // DIFF_UNROLL

    def trip(jj, carry):
        for u in range(DIFF_UNROLL):
            carry = step(DIFF_UNROLL * jj + 1 + u, (1 + u) % 2, carry)
        return carry

    carry = lax.fori_loop(0, n_trips, trip, carry)
    for j in range(1 + DIFF_UNROLL * n_trips, nk - 1):
        carry = step(j, j % 2, carry)
    m, a_prev, c_last = carry
    last = (nk - 1) % 2
    m, a_last = softmax(s_refs[last], e_refs[last], c_last, m)
    accumulate(nk - 2, e_refs[1 - last], a_prev)
    accumulate(nk - 1, e_refs[last], a_last)

    acc = acc_ref[...]
    o = acc[:DIFF_VDIM] / acc[DIFF_VDIM:DIFF_VDIM + 1]
    lp = lam_ref[...]
    lam = (jnp.exp(jnp.sum(lp[0:1] * lp[1:2], axis=-1, keepdims=True))
           - jnp.exp(jnp.sum(lp[2:3] * lp[3:4], axis=-1, keepdims=True)) + lambda_init)
    out = o[:, :tq] - lam * o[:, tq:]
    ms = jnp.mean(out * out, axis=0, keepdims=True)
    y = out * lax.rsqrt(ms + EPS) * g_ref[...] * (1.0 - lambda_init)
    o_ref[...] = y.T.astype(o_ref.dtype)


def _diff_bias(rel_bias):
    t = TB_DIFF
    assert t >= MAX_DISTANCE
    kk = jnp.arange(t)[:, None]
    qq = jnp.arange(t)[None, :]
    rel = jnp.stack([d * t + kk - qq for d in range(-2, 3)])
    return (_bias_lookup(rel_bias, rel) * LOG2E).astype(BF16)


def _diff_attention(qt, k, vt, bias, diff_lambda, subln_g, seq_len, lambda_init):
    n = k.shape[0]
    tq, tk = TQ_DIFF, TK_DIFF
    assert seq_len % tk == 0 and (seq_len // tk) % 2 == 0 and seq_len // tk >= 2
    assert tq % TB_DIFF == 0 and tk % TB_DIFF == 0
    nb = n // seq_len
    nq = seq_len // tq
    nk = seq_len // tk
    return pl.pallas_call(
        functools.partial(_diff_kernel, lambda_init=lambda_init),
        grid=(nb, DIFF_HEADS, nq),
        in_specs=[pl.BlockSpec((DIFF_VDIM, tq), lambda b, h, i: (h, b * nq + i)),
                  pl.BlockSpec((seq_len, DIFF_VDIM), lambda b, h, i: (b, h)),
                  pl.BlockSpec((nk, DIFF_VDIM, tk), lambda b, h, i: (b, h, 0)),
                  pl.BlockSpec((None, 5, TB_DIFF, TB_DIFF), lambda b, h, i: (h, 0, 0, 0)),
                  pl.BlockSpec((4, HEAD_DIM), lambda b, h, i: (0, 0)),
                  pl.BlockSpec((DIFF_VDIM, 1), lambda b, h, i: (0, 0))],
        out_specs=pl.BlockSpec((tq, DIFF_VDIM), lambda b, h, i: (b * nq + i, h)),
        out_shape=jax.ShapeDtypeStruct((n, DIFF_HEADS * DIFF_VDIM), BF16),
        scratch_shapes=[pltpu.VMEM((DIFF_VDIM, 2 * tq), BF16),
                        pltpu.VMEM((tk, 2 * tq), BF16),
                        pltpu.VMEM((tk, 2 * tq), BF16),
                        pltpu.VMEM((tk, 2 * tq), BF16),
                        pltpu.VMEM((tk, 2 * tq), BF16),
                        pltpu.VMEM((DIFF_VDIM + BF16_SUBLANES, 2 * tq), F32)],
        compiler_params=_cparams("parallel", "parallel", "arbitrary"),
        name="diff_attention",
    )(qt, k, vt, bias, diff_lambda.astype(F32), subln_g.reshape(DIFF_VDIM, 1).astype(F32))


def _router_kernel(x_ref, gi_ref, rw_ref, rb_ref, ti_ref, tw_ref):
    h = _rms(x_ref[...], gi_ref[...]).astype(BF16)
    logits = jnp.dot(h, rw_ref[...], preferred_element_type=F32) + rb_ref[...]
    col = lax.broadcasted_iota(jnp.int32, logits.shape, 1)
    m1 = jnp.max(logits, axis=-1, keepdims=True)
    i1 = jnp.min(jnp.where(logits == m1, col, LANES), axis=-1, keepdims=True)
    rest = jnp.where(col == i1, 2 * NEG, logits)
    m2 = jnp.max(rest, axis=-1, keepdims=True)
    i2 = jnp.min(jnp.where(rest == m2, col, LANES), axis=-1, keepdims=True)
    e2 = jnp.exp(m2 - m1)
    den = 1.0 + e2
    ti_ref[...] = jnp.concatenate([i1, i2], axis=1)
    tw_ref[...] = jnp.concatenate([1.0 / den, e2 / den], axis=1)


def _router(x, g_in, router_w, router_b):
    n, d = x.shape
    ne = router_w.shape[1]
    rw = jnp.zeros((d, LANES), BF16).at[:, :ne].set(router_w.astype(BF16))
    rb = jnp.full((1, LANES), NEG, F32).at[0, :ne].set(router_b.astype(F32))
    return pl.pallas_call(
        _router_kernel,
        grid=(n // TM,),
        in_specs=[pl.BlockSpec((TM, d), lambda i: (i, 0)),
                  pl.BlockSpec((1, d), lambda i: (0, 0)),
                  pl.BlockSpec((d, LANES), lambda i: (0, 0)),
                  pl.BlockSpec((1, LANES), lambda i: (0, 0))],
        out_specs=[pl.BlockSpec((TM, 2), lambda i: (i, 0)), pl.BlockSpec((TM, 2), lambda i: (i, 0))],
        out_shape=[jax.ShapeDtypeStruct((n, 2), jnp.int32), jax.ShapeDtypeStruct((n, 2), F32)],
        compiler_params=_cparams("parallel"),
        name="moe_router",
    )(x, g_in.reshape(1, d), rw, rb)


def _route(topi, topw, ne, t):
    n = topi.shape[0]
    p = 2 * n
    e = topi.reshape(p)
    onehot = (e[:, None] == jnp.arange(ne)[None, :]).astype(jnp.int32)
    csum = jnp.cumsum(onehot, axis=0)
    rank = jnp.sum(onehot * csum, axis=1) - 1
    padded = ((csum[-1] + t - 1) // t) * t
    ends = jnp.cumsum(padded)
    dest = jnp.sum(onehot * (ends - padded)[None, :], axis=1) + rank
    n_tiles = p // t + ne
    src = jnp.zeros((n_tiles * t,), jnp.int32).at[dest].set(jnp.arange(p, dtype=jnp.int32) // 2)
    gate = jnp.zeros((n_tiles * t,), F32).at[dest].set(topw.reshape(p))
    tile_expert = jnp.sum(jnp.arange(n_tiles, dtype=jnp.int32)[:, None] * t >= ends[None, :], axis=1)
    tile_expert = jnp.minimum(tile_expert, ne - 1).astype(jnp.int32)
    dest_tiles = dest.reshape(n // TM, TM, 2).transpose(0, 2, 1).reshape(n // TM, 2 * TM)
    return src.reshape(n_tiles, t), gate.reshape(n_tiles * t, 1), tile_expert, dest_tiles.astype(jnp.int32)


def _row_copy(src_hbm, row, dst_ref, r, sem):
    return pltpu.make_async_copy(src_hbm.at[pl.ds(row, 1)], dst_ref.at[pl.ds(r, 1)], sem)


def _start_row_gather(idx_ref, islot, base, n_rows, src_hbm, dst_ref, sem):
    def body(r, c):
        _row_copy(src_hbm, idx_ref[islot, base + r], dst_ref, r, sem).start()
        return c
    lax.fori_loop(0, n_rows, body, 0, unroll=8)


def _wait_row_gather(n_rows, src_hbm, dst_ref, sem):
    def body(r, c):
        _row_copy(src_hbm, 0, dst_ref, 0, sem).wait()
        return c
    lax.fori_loop(0, n_rows, body, 0, unroll=8)


def _prefetch_schedule(t, n_steps, idx_hbm, idx_ref, isem, start_gather):
    def idx_copy(step):
        return pltpu.make_async_copy(idx_hbm.at[step], idx_ref.at[step % 3], isem.at[step % 3])

    @pl.when(t == 0)
    def _():
        idx_copy(0).start()
        idx_copy(0).wait()
        start_gather(0)
        if n_steps > 1:
            idx_copy(1).start()

    @pl.when(t + 1 < n_steps)
    def _():
        idx_copy(t + 1).wait()

        @pl.when(t + 2 < n_steps)
        def _():
            idx_copy(t + 2).start()

        start_gather(t + 1)


def _moe_expert_kernel(te_ref, src_hbm, gate_ref, x_hbm, gi_ref, wg_ref, wu_ref, wd_ref, o_ref,
                       idx_ref, xg_ref, isem, gsem, h_ref, acc_ref):
    t = pl.program_id(0)
    j = pl.program_id(1)
    rows = xg_ref.shape[1]

    @pl.when(j == 0)
    def _():
        def start_gather(step):
            _start_row_gather(idx_ref, step % 3, 0, rows, x_hbm, xg_ref.at[step % 2], gsem.at[step % 2])

        _prefetch_schedule(t, src_hbm.shape[0], src_hbm, idx_ref, isem, start_gather)
        _wait_row_gather(rows, x_hbm, xg_ref.at[t % 2], gsem.at[t % 2])
        h_ref[...] = _rms(xg_ref[t % 2], gi_ref[...]).astype(BF16)
        acc_ref[...] = jnp.zeros_like(acc_ref)

    h = h_ref[...]
    a = jnp.dot(h, wg_ref[...], preferred_element_type=F32)
    b = jnp.dot(h, wu_ref[...], preferred_element_type=F32)
    mid = (a * jax.nn.sigmoid(a) * b).astype(BF16)
    acc_ref[...] += jnp.dot(mid, wd_ref[...], preferred_element_type=F32)

    @pl.when(j == pl.num_programs(1) - 1)
    def _():
        o_ref[...] = gate_ref[...] * acc_ref[...]


def _moe_experts(x, g_in, src, gate, tile_expert, wg, wu, wd):
    n, d = x.shape
    n_tiles, t = src.shape
    nf = wg.shape[2] // TF
    return pl.pallas_call(
        _moe_expert_kernel,
        grid_spec=pltpu.PrefetchScalarGridSpec(
            num_scalar_prefetch=1,
            grid=(n_tiles, nf),
            in_specs=[pl.BlockSpec(memory_space=pl.ANY),
                      pl.BlockSpec((t, 1), lambda i, j, te: (i, 0)),
                      pl.BlockSpec(memory_space=pl.ANY),
                      pl.BlockSpec((1, d), lambda i, j, te: (0, 0)),
                      pl.BlockSpec((None, d, TF), lambda i, j, te: (te[i], 0, j)),
                      pl.BlockSpec((None, d, TF), lambda i, j, te: (te[i], 0, j)),
                      pl.BlockSpec((None, TF, d), lambda i, j, te: (te[i], j, 0))],
            out_specs=pl.BlockSpec((t, d), lambda i, j, te: (i, 0)),
            scratch_shapes=[pltpu.SMEM((3, t), jnp.int32),
                            pltpu.VMEM((2, t, d), F32),
                            pltpu.SemaphoreType.DMA((3,)),
                            pltpu.SemaphoreType.DMA((2,)),
                            pltpu.VMEM((t, d), BF16),
                            pltpu.VMEM((t, d), F32)]),
        out_shape=jax.ShapeDtypeStruct((n_tiles * t, d), F32),
        compiler_params=_cparams("arbitrary", "arbitrary"),
        name="moe_experts",
    )(tile_expert, src, gate, x, g_in.reshape(1, d), wg, wu, wd)


def _moe_combine_kernel(dest_hbm, y_hbm, x_ref, go_ref, o_ref, idx_ref, buf_ref, isem, gsem):
    i = pl.program_id(0)
    tm = x_ref.shape[0]

    def start_gather(step):
        for slot in range(2):
            _start_row_gather(idx_ref, step % 3, slot * tm, tm, y_hbm,
                              buf_ref.at[step % 2, slot], gsem.at[step % 2])

    _prefetch_schedule(i, dest_hbm.shape[0], dest_hbm, idx_ref, isem, start_gather)
    for slot in range(2):
        _wait_row_gather(tm, y_hbm, buf_ref.at[i % 2, slot], gsem.at[i % 2])
    y = buf_ref[i % 2, 0] + buf_ref[i % 2, 1]
    o_ref[...] = x_ref[...] + _rms(y, go_ref[...])


def _moe_combine(y, dest_tiles, x, g_out):
    n, d = x.shape
    return pl.pallas_call(
        _moe_combine_kernel,
        grid=(n // TM,),
        in_specs=[pl.BlockSpec(memory_space=pl.ANY),
                  pl.BlockSpec(memory_space=pl.ANY),
                  pl.BlockSpec((TM, d), lambda i: (i, 0)),
                  pl.BlockSpec((1, d), lambda i: (0, 0))],
        out_specs=pl.BlockSpec((TM, d), lambda i: (i, 0)),
        out_shape=jax.ShapeDtypeStruct((n, d), F32),
        scratch_shapes=[pltpu.SMEM((3, 2 * TM), jnp.int32),
                        pltpu.VMEM((2, 2, TM, d), F32),
                        pltpu.SemaphoreType.DMA((3,)),
                        pltpu.SemaphoreType.DMA((2,))],
        compiler_params=_cparams("arbitrary"),
        name="moe_combine",
    )(dest_tiles, y, x, g_out.reshape(1, d))


def _moe(x, g_in, router_w, router_b, wg, wu, wd, g_out):
    topi, topw = _router(x, g_in, router_w, router_b)
    src, gate, tile_expert, dest_tiles = _route(topi, topw, wg.shape[0], TM)
    y = _moe_experts(x, g_in, src, gate, tile_expert, wg, wu, wd)
    return _moe_combine(y, dest_tiles, x, g_out)


def _trunk(x, seq_len, p):
    n = x.shape[0]
    depth = p["norm_mix"].shape[0]
    for layer in range(depth):
        j = layer // 2
        if layer % 2 == 0:
            o_q = POOL_WIDTH
            o_kv = o_q + WIN_Q_HEADS * HEAD_DIM
            o_end = o_kv + 2 * WIN_KV_HEADS * HEAD_DIM
            u, q, kv = _norm_matmul(x, p["norm_mix"][layer, 0], p["w_in0"][j],
                                    ((0, o_q), (o_q, o_kv), (o_kv, o_end)), (F32, BF16, BF16))
            a = _pool_mixer(u, p["pool_w"][j], p["pool_scale"][j], seq_len)
            b = _window_attention(q, kv, p["win_bias"], p["sink"][j], seq_len)
            w_out = p["w_out0"][j]
            x = _proj_norm_res([a, b], [w_out[:POOL_WIDTH], w_out[POOL_WIDTH:]], x, p["norm_mix"][layer, 1])
            x = _ffn(x, p["norm_ffn"][layer, 0], p["ffn_wg"][j], p["ffn_wu"][j], p["ffn_wd"][j],
                     p["norm_ffn"][layer, 1])
        else:
            lambda_init = 0.8 - 0.6 * math.exp(-0.3 * layer)
            dq = DIFF_HEADS * 2 * HEAD_DIM
            q, k, v = _norm_matmul(x, p["norm_mix"][layer, 0], p["w_in1"][j],
                                   ((0, dq), (dq, 2 * dq), (2 * dq, 3 * dq)), (BF16, BF16, BF16))
            qt = q.T
            vt = v.reshape(n // TK_DIFF, TK_DIFF, dq).transpose(0, 2, 1)
            m = _diff_attention(qt, k, vt, p["diff_bias"], p["diff_lambda"][j], p["subln_g"][j],
                                seq_len, lambda_init)
            x = _proj_norm_res([m], [p["w_out1"][j]], x, p["norm_mix"][layer, 1])
            x = _moe(x, p["norm_ffn"][layer, 0], p["router_w"][j], p["router_b"][j],
                     p["moe_wg"][j], p["moe_wu"][j], p["moe_wd"][j], p["norm_ffn"][layer, 1])
    return x


def _prepare(norm_mix, norm_ffn, rel_bias, w_in0, pool_w, pool_scale, sink, w_out0,
             ffn_wg, ffn_wu, ffn_wd, w_in1, diff_lambda, subln_g, w_out1,
             router_w, router_b, moe_wg, moe_wu, moe_wd):
    scale = HEAD_DIM ** -0.5
    o_q = POOL_WIDTH
    o_kv = o_q + WIN_Q_HEADS * HEAD_DIM
    col0 = jnp.arange(w_in0.shape[-1])
    s0 = jnp.where((col0 >= o_q) & (col0 < o_kv), scale, 1.0).astype(F32)
    col1 = jnp.arange(w_in1.shape[-1])
    s1 = jnp.where(col1 < DIFF_HEADS * 2 * HEAD_DIM, scale * LOG2E, 1.0).astype(F32)
    return dict(
        norm_mix=norm_mix.astype(F32), norm_ffn=norm_ffn.astype(F32),
        win_bias=_window_bias(rel_bias, TQ_WIN), diff_bias=_diff_bias(rel_bias),
        w_in0=(w_in0 * s0).astype(BF16), pool_w=pool_w.astype(BF16), pool_scale=pool_scale.astype(F32),
        sink=sink, w_out0=w_out0.astype(BF16),
        ffn_wg=ffn_wg.astype(BF16), ffn_wu=ffn_wu.astype(BF16), ffn_wd=ffn_wd.astype(BF16),
        w_in1=(w_in1 * s1).astype(BF16), diff_lambda=diff_lambda, subln_g=subln_g,
        w_out1=w_out1.astype(BF16), router_w=router_w, router_b=router_b,
        moe_wg=moe_wg.astype(BF16), moe_wu=moe_wu.astype(BF16), moe_wd=moe_wd.astype(BF16))


def kernel(x_prompt, x_sample, norm_mix, norm_ffn, rel_bias, w_in0, pool_w, pool_scale, sink, w_out0,
           ffn_wg, ffn_wu, ffn_wd, w_in1, diff_lambda, subln_g, w_out1,
           router_w, router_b, moe_wg, moe_wu, moe_wd):
    p = _prepare(norm_mix, norm_ffn, rel_bias, w_in0, pool_w, pool_scale, sink, w_out0,
                 ffn_wg, ffn_wu, ffn_wd, w_in1, diff_lambda, subln_g, w_out1,
                 router_w, router_b, moe_wg, moe_wu, moe_wd)
    outs = []
    for x in (x_prompt, x_sample):
        b, s, d = x.shape
        outs.append(_trunk(x.reshape(b * s, d), s, p).reshape(b, s, d))
    return tuple(outs)
```

```python
import functools
import math

import jax
import jax.numpy as jnp
from jax import lax
from jax.experimental import pallas as pl
from jax.experimental.pallas import tpu as pltpu

D_MODEL = 1024
HEAD_DIM = 64
POOL_WIDTH = 512
POOL_WINDOWS = (2, 4, 8, 16)
POOL_GW = 128
POOL_HALO = 8
WIN_Q_HEADS = 8
WIN_KV_HEADS = 2
WIN_GROUP = 4
WINDOW = 128
DIFF_HEADS = 8
DIFF_VDIM = 128
NUM_BUCKETS = 32
MAX_DISTANCE = 128
D_FF = 2816
N_EXPERTS = 8
EPS = 1e-6
NEG = -1e30
LOG2E = 1.4426950408889634

LANES = 128
BF16_SUBLANES = 16
VMEM_LIMIT = 56 * 1024 * 1024

TM = 512
TF = 1408
TQ_WIN = 256
TQ_DIFF = 512
TK_DIFF = 512
TB_DIFF = 256
DIFF_UNROLL = 4

F32 = jnp.float32
BF16 = jnp.bfloat16


def _cparams(*sem):
    return pltpu.CompilerParams(dimension_semantics=sem, vmem_limit_bytes=VMEM_LIMIT)


def _rms(xf, g):
    ms = jnp.mean(xf * xf, axis=-1, keepdims=True)
    return xf * lax.rsqrt(ms + EPS) * g


def _t5_bucket(rel):
    half = NUM_BUCKETS // 2
    max_exact = half // 2
    ret = jnp.where(rel > 0, half, 0)
    n = jnp.abs(rel)
    nf = jnp.maximum(n, 1).astype(jnp.float32)
    large = max_exact + (jnp.log(nf / max_exact) / math.log(MAX_DISTANCE / max_exact)
                         * (half - max_exact)).astype(jnp.int32)
    large = jnp.minimum(large, half - 1)
    return ret + jnp.where(n < max_exact, n, large)


def _bias_lookup(rel_bias, rel):
    onehot = (_t5_bucket(rel)[..., None] == jnp.arange(NUM_BUCKETS)).astype(F32)
    out = jnp.einsum("...b,bh->h...", onehot, rel_bias.astype(F32), precision=lax.Precision.HIGHEST)
    return out


def _norm_matmul_kernel(x_ref, g_ref, w_ref, *o_refs, splits):
    h = _rms(x_ref[...], g_ref[...]).astype(BF16)
    z = jnp.dot(h, w_ref[...], preferred_element_type=F32)
    for o_ref, (a, b) in zip(o_refs, splits):
        o_ref[...] = z[:, a:b].astype(o_ref.dtype)


def _norm_matmul(x, g, w, splits, dtypes):
    n, d = x.shape
    nout = w.shape[1]
    return pl.pallas_call(
        functools.partial(_norm_matmul_kernel, splits=splits),
        grid=(n // TM,),
        in_specs=[pl.BlockSpec((TM, d), lambda i: (i, 0)),
                  pl.BlockSpec((1, d), lambda i: (0, 0)),
                  pl.BlockSpec((d, nout), lambda i: (0, 0))],
        out_specs=[pl.BlockSpec((TM, b - a), lambda i: (i, 0)) for a, b in splits],
        out_shape=[jax.ShapeDtypeStruct((n, b - a), dt) for (a, b), dt in zip(splits, dtypes)],
        compiler_params=_cparams("parallel"),
        name="norm_matmul",
    )(x, g.reshape(1, d), w)


def _proj_norm_res_kernel(*refs, n_in):
    ins, ws = refs[:n_in], refs[n_in:2 * n_in]
    x_ref, g_ref, o_ref = refs[2 * n_in:]
    m = jnp.dot(ins[0][...], ws[0][...], preferred_element_type=F32)
    for a_ref, w_ref in zip(ins[1:], ws[1:]):
        m = m + jnp.dot(a_ref[...], w_ref[...], preferred_element_type=F32)
    o_ref[...] = x_ref[...] + _rms(m, g_ref[...])


def _proj_norm_res(ins, ws, x, g):
    n, d = x.shape
    n_in = len(ins)
    return pl.pallas_call(
        functools.partial(_proj_norm_res_kernel, n_in=n_in),
        grid=(n // TM,),
        in_specs=([pl.BlockSpec((TM, a.shape[1]), lambda i: (i, 0)) for a in ins]
                  + [pl.BlockSpec(w.shape, lambda i: (0, 0)) for w in ws]
                  + [pl.BlockSpec((TM, d), lambda i: (i, 0)),
                     pl.BlockSpec((1, d), lambda i: (0, 0))]),
        out_specs=pl.BlockSpec((TM, d), lambda i: (i, 0)),
        out_shape=jax.ShapeDtypeStruct((n, d), F32),
        compiler_params=_cparams("parallel"),
        name="proj_norm_res",
    )(*ins, *ws, x, g.reshape(1, d))


def _pool_kernel(up_ref, u_ref, un_ref, w_ref, sc_ref, o_ref, ext_ref, *, seq_len):
    tm = u_ref.shape[0]
    t0 = (pl.program_id(0) * tm) % seq_len
    ext_ref[0:POOL_HALO, :] = jnp.where(t0 > 0, up_ref[...], 0.0)
    ext_ref[POOL_HALO:POOL_HALO + tm, :] = u_ref[...]
    ext_ref[POOL_HALO + tm:, :] = jnp.where(t0 + tm < seq_len, un_ref[...], 0.0)
    t = t0 + lax.broadcasted_iota(jnp.int32, (tm, 1), 0)
    outs = []
    for g, w in enumerate(POOL_WINDOWS):
        half = w // 2
        sl = slice(g * POOL_GW, (g + 1) * POOL_GW)
        acc = ext_ref[POOL_HALO - half:POOL_HALO - half + tm, sl]
        for j in range(-half + 1, half):
            acc = acc + ext_ref[POOL_HALO + j:POOL_HALO + j + tm, sl]
        lo = jnp.maximum(t - half, 0)
        hi = jnp.minimum(t + half - 1, seq_len - 1)
        cnt = (hi - lo + 1).astype(F32)
        d = (acc / cnt - u_ref[:, sl]).astype(BF16)
        outs.append(jnp.dot(d, w_ref[g], preferred_element_type=F32))
    o_ref[...] = (jnp.concatenate(outs, axis=-1) * sc_ref[...]).astype(o_ref.dtype)


def _pool_mixer(u, pool_w, pool_scale, seq_len):
    n, c = u.shape
    hb = TM // POOL_HALO
    last = n // POOL_HALO - 1
    return pl.pallas_call(
        functools.partial(_pool_kernel, seq_len=seq_len),
        grid=(n // TM,),
        in_specs=[pl.BlockSpec((POOL_HALO, c), lambda i: (jnp.maximum(i * hb - 1, 0), 0)),
                  pl.BlockSpec((TM, c), lambda i: (i, 0)),
                  pl.BlockSpec((POOL_HALO, c), lambda i: (jnp.minimum((i + 1) * hb, last), 0)),
                  pl.BlockSpec(pool_w.shape, lambda i: (0, 0, 0)),
                  pl.BlockSpec((1, c), lambda i: (0, 0))],
        out_specs=pl.BlockSpec((TM, c), lambda i: (i, 0)),
        out_shape=jax.ShapeDtypeStruct((n, c), BF16),
        scratch_shapes=[pltpu.VMEM((TM + 2 * POOL_HALO, c), F32)],
        compiler_params=_cparams("parallel"),
        name="pool_mixer",
    )(u, u, u, pool_w, pool_scale.reshape(1, c))


def _win_kernel(q_ref, kvp_ref, kv_ref, kvn_ref, bias_ref, sink_ref, o_ref, *, seq_len):
    tq = q_ref.shape[0]
    nk = tq + 2 * WINDOW
    t0 = (pl.program_id(0) * tq) % seq_len
    kv = jnp.concatenate([kvp_ref[...], kv_ref[...], kvn_ref[...]], axis=0)
    kpos = t0 - WINDOW + lax.broadcasted_iota(jnp.int32, (1, nk), 1)
    valid = (kpos >= 0) & (kpos < seq_len)
    q = q_ref[...]
    sink = sink_ref[...]
    outs = []
    for g in range(WIN_KV_HEADS):
        k = kv[:, g * HEAD_DIM:(g + 1) * HEAD_DIM]
        v = kv[:, (WIN_KV_HEADS + g) * HEAD_DIM:(WIN_KV_HEADS + g + 1) * HEAD_DIM]
        for r in range(WIN_GROUP):
            h = g * WIN_GROUP + r
            qh = q[:, h * HEAD_DIM:(h + 1) * HEAD_DIM]
            s = lax.dot_general(qh, k, (((1,), (1,)), ((), ())), preferred_element_type=F32)
            s = jnp.where(valid, s + bias_ref[h], NEG)
            sk = sink[:, h:h + 1]
            m = jnp.maximum(jnp.max(s, axis=-1, keepdims=True), sk)
            e = jnp.exp(s - m)
            l = jnp.sum(e, axis=-1, keepdims=True) + jnp.exp(sk - m)
            p = (e / l).astype(BF16)
            outs.append(jnp.dot(p, v, preferred_element_type=F32))
    o_ref[...] = jnp.concatenate(outs, axis=-1).astype(o_ref.dtype)


def _window_bias(rel_bias, tq):
    nk = tq + 2 * WINDOW
    rel = jnp.arange(nk)[None, :] - WINDOW - jnp.arange(tq)[:, None]
    return jnp.where((jnp.abs(rel) <= WINDOW)[None], _bias_lookup(rel_bias, rel), NEG)


def _window_attention(q, kv, bias, sink, seq_len):
    n = q.shape[0]
    tq = TQ_WIN
    hb = tq // WINDOW
    last = n // WINDOW - 1
    return pl.pallas_call(
        functools.partial(_win_kernel, seq_len=seq_len),
        grid=(n // tq,),
        in_specs=[pl.BlockSpec((tq, q.shape[1]), lambda i: (i, 0)),
                  pl.BlockSpec((WINDOW, kv.shape[1]), lambda i: (jnp.maximum(i * hb - 1, 0), 0)),
                  pl.BlockSpec((tq, kv.shape[1]), lambda i: (i, 0)),
                  pl.BlockSpec((WINDOW, kv.shape[1]), lambda i: (jnp.minimum((i + 1) * hb, last), 0)),
                  pl.BlockSpec(bias.shape, lambda i: (0, 0, 0)),
                  pl.BlockSpec((1, WIN_Q_HEADS), lambda i: (0, 0))],
        out_specs=pl.BlockSpec((tq, q.shape[1]), lambda i: (i, 0)),
        out_shape=jax.ShapeDtypeStruct(q.shape, BF16),
        compiler_params=_cparams("parallel"),
        name="window_attention",
    )(q, kv, kv, kv, bias, sink.reshape(1, WIN_Q_HEADS).astype(F32))


def _ffn_kernel(x_ref, gi_ref, wg_ref, wu_ref, wd_ref, go_ref, o_ref, h_ref, acc_ref):
    j = pl.program_id(1)

    @pl.when(j == 0)
    def _():
        h_ref[...] = _rms(x_ref[...], gi_ref[...]).astype(BF16)
        acc_ref[...] = jnp.zeros_like(acc_ref)

    h = h_ref[...]
    a = jnp.dot(h, wg_ref[...], preferred_element_type=F32)
    b = jnp.dot(h, wu_ref[...], preferred_element_type=F32)
    mid = (a * jax.nn.sigmoid(a) * b).astype(BF16)
    acc_ref[...] += jnp.dot(mid, wd_ref[...], preferred_element_type=F32)

    @pl.when(j == pl.num_programs(1) - 1)
    def _():
        o_ref[...] = x_ref[...] + _rms(acc_ref[...], go_ref[...])


def _ffn(x, g_in, wg, wu, wd, g_out):
    n, d = x.shape
    nf = wg.shape[1] // TF
    return pl.pallas_call(
        _ffn_kernel,
        grid=(n // TM, nf),
        in_specs=[pl.BlockSpec((TM, d), lambda i, j: (i, 0)),
                  pl.BlockSpec((1, d), lambda i, j: (0, 0)),
                  pl.BlockSpec((d, TF), lambda i, j: (0, j)),
                  pl.BlockSpec((d, TF), lambda i, j: (0, j)),
                  pl.BlockSpec((TF, d), lambda i, j: (j, 0)),
                  pl.BlockSpec((1, d), lambda i, j: (0, 0))],
        out_specs=pl.BlockSpec((TM, d), lambda i, j: (i, 0)),
        out_shape=jax.ShapeDtypeStruct((n, d), F32),
        scratch_shapes=[pltpu.VMEM((TM, d), BF16), pltpu.VMEM((TM, d), F32)],
        compiler_params=_cparams("parallel", "arbitrary"),
        name="ffn",
    )(x, g_in.reshape(1, d), wg, wu, wd, g_out.reshape(1, d))


def _diff_kernel(qt_ref, k_ref, vt_ref, bias_ref, lam_ref, g_ref, o_ref,
                 qbd_ref, s0_ref, s1_ref, e0_ref, e1_ref, acc_ref, *, lambda_init):
    tq = qt_ref.shape[1]
    nk, _, tk = vt_ref.shape
    i = pl.program_id(2)
    qt = qt_ref[...]
    zero = jnp.zeros((HEAD_DIM, tq), BF16)
    qbd_ref[0:HEAD_DIM, 0:tq] = qt[0:HEAD_DIM]
    qbd_ref[0:HEAD_DIM, tq:] = zero
    qbd_ref[HEAD_DIM:, 0:tq] = zero
    qbd_ref[HEAD_DIM:, tq:] = qt[HEAD_DIM:]
    acc_ref[...] = jnp.zeros_like(acc_ref)
    ones = jnp.ones((BF16_SUBLANES, tk), BF16)

    def scores(j, dst_ref):
        kc = k_ref[pl.ds(pl.multiple_of(j * tk, tk), tk), :]
        s = jnp.dot(kc, qbd_ref[...], preferred_element_type=F32).astype(BF16)
        rows = []
        for c in range(tk // TB_DIFF):
            sc = s[c * TB_DIFF:(c + 1) * TB_DIFF]
            cols = []
            for mp in range(2):
                for a in range(tq // TB_DIFF):
                    d = jnp.clip(j * (tk // TB_DIFF) + c - (i * (tq // TB_DIFF) + a), -2, 2) + 2
                    c0 = mp * tq + a * TB_DIFF
                    cols.append(sc[:, c0:c0 + TB_DIFF] + bias_ref[d])
            rows.append(jnp.concatenate(cols, axis=1))
        s = jnp.concatenate(rows, axis=0)
        dst_ref[...] = s
        return jnp.max(s, axis=0, keepdims=True)

    def softmax(src_ref, dst_ref, cmax, m_old):
        m_new = jnp.maximum(m_old, cmax.astype(F32))
        alpha = jnp.exp2(m_old - m_new)
        dst_ref[...] = jnp.exp2(src_ref[...] - m_new.astype(BF16))
        return m_new, alpha

    def accumulate(j, src_ref, alpha):
        vt = jnp.concatenate([vt_ref[j], ones], axis=0)
        acc_ref[...] = alpha * acc_ref[...] + jnp.dot(vt, src_ref[...], preferred_element_type=F32)

    s_refs = (s0_ref, s1_ref)
    e_refs = (e0_ref, e1_ref)

    def step(j, parity, carry):
        m, a_prev, c_cur = carry
        c_next = scores(j + 1, s_refs[1 - parity])
        m, a_cur = softmax(s_refs[parity], e_refs[parity], c_cur, m)
        accumulate(j - 1, e_refs[1 - parity], a_prev)
        return m, a_cur, c_next

    m = jnp.full((1, 2 * tq), NEG, F32)
    c0 = scores(0, s0_ref)
    c1 = scores(1, s1_ref)
    m, a0 = softmax(s0_ref, e0_ref, c0, m)
    carry = (m, a0, c1)

    n_trips = (nk - 2) // DIFF_UNROLL

    def trip(jj, carry):
        for u in range(DIFF_UNROLL):
            carry = step(DIFF_UNROLL * jj + 1 + u, (1 + u) % 2, carry)
        return carry

    carry = lax.fori_loop(0, n_trips, trip, carry)
    for j in range(1 + DIFF_UNROLL * n_trips, nk - 1):
        carry = step(j, j % 2, carry)
    m, a_prev, c_last = carry
    last = (nk - 1) % 2
    m, a_last = softmax(s_refs[last], e_refs[last], c_last, m)
    accumulate(nk - 2, e_refs[1 - last], a_prev)
    accumulate(nk - 1, e_refs[last], a_last)

    acc = acc_ref[...]
    o = acc[:DIFF_VDIM] / acc[DIFF_VDIM:DIFF_VDIM + 1]
    lp = lam_ref[...]
    lam = (jnp.exp(jnp.sum(lp[0:1] * lp[1:2], axis=-1, keepdims=True))
           - jnp.exp(jnp.sum(lp[2:3] * lp[3:4], axis=-1, keepdims=True)) + lambda_init)
    out = o[:, :tq] - lam * o[:, tq:]
    ms = jnp.mean(out * out, axis=0, keepdims=True)
    y = out * lax.rsqrt(ms + EPS) * g_ref[...] * (1.0 - lambda_init)
    o_ref[...] = y.T.astype(o_ref.dtype)


def _diff_bias(rel_bias):
    t = TB_DIFF
    assert t >= MAX_DISTANCE
    kk = jnp.arange(t)[:, None]
    qq = jnp.arange(t)[None, :]
    rel = jnp.stack([d * t + kk - qq for d in range(-2, 3)])
    return (_bias_lookup(rel_bias, rel) * LOG2E).astype(BF16)


def _diff_attention(qt, k, vt, bias, diff_lambda, subln_g, seq_len, lambda_init):
    n = k.shape[0]
    tq, tk = TQ_DIFF, TK_DIFF
    assert seq_len % tk == 0 and (seq_len // tk) % 2 == 0 and seq_len // tk >= 2
    assert tq % TB_DIFF == 0 and tk % TB_DIFF == 0
    nb = n // seq_len
    nq = seq_len // tq
    nk = seq_len // tk
    return pl.pallas_call(
        functools.partial(_diff_kernel, lambda_init=lambda_init),
        grid=(nb, DIFF_HEADS, nq),
        in_specs=[pl.BlockSpec((DIFF_VDIM, tq), lambda b, h, i: (h, b * nq + i)),
                  pl.BlockSpec((seq_len, DIFF_VDIM), lambda b, h, i: (b, h)),
                  pl.BlockSpec((nk, DIFF_VDIM, tk), lambda b, h, i: (b, h, 0)),
                  pl.BlockSpec((None, 5, TB_DIFF, TB_DIFF), lambda b, h, i: (h, 0, 0, 0)),
                  pl.BlockSpec((4, HEAD_DIM), lambda b, h, i: (0, 0)),
                  pl.BlockSpec((DIFF_VDIM, 1), lambda b, h, i: (0, 0))],
        out_specs=pl.BlockSpec((tq, DIFF_VDIM), lambda b, h, i: (b * nq + i, h)),
        out_shape=jax.ShapeDtypeStruct((n, DIFF_HEADS * DIFF_VDIM), BF16),
        scratch_shapes=[pltpu.VMEM((DIFF_VDIM, 2 * tq), BF16),
                        pltpu.VMEM((tk, 2 * tq), BF16),
                        pltpu.VMEM((tk, 2 * tq), BF16),
                        pltpu.VMEM((tk, 2 * tq), BF16),
                        pltpu.VMEM((tk, 2 * tq), BF16),
                        pltpu.VMEM((DIFF_VDIM + BF16_SUBLANES, 2 * tq), F32)],
        compiler_params=_cparams("parallel", "parallel", "arbitrary"),
        name="diff_attention",
    )(qt, k, vt, bias, diff_lambda.astype(F32), subln_g.reshape(DIFF_VDIM, 1).astype(F32))


def _router_kernel(x_ref, gi_ref, rw_ref, rb_ref, ti_ref, tw_ref):
    h = _rms(x_ref[...], gi_ref[...]).astype(BF16)
    logits = jnp.dot(h, rw_ref[...], preferred_element_type=F32) + rb_ref[...]
    col = lax.broadcasted_iota(jnp.int32, logits.shape, 1)
    m1 = jnp.max(logits, axis=-1, keepdims=True)
    i1 = jnp.min(jnp.where(logits == m1, col, LANES), axis=-1, keepdims=True)
    rest = jnp.where(col == i1, 2 * NEG, logits)
    m2 = jnp.max(rest, axis=-1, keepdims=True)
    i2 = jnp.min(jnp.where(rest == m2, col, LANES), axis=-1, keepdims=True)
    e2 = jnp.exp(m2 - m1)
    den = 1.0 + e2
    ti_ref[...] = jnp.concatenate([i1, i2], axis=1)
    tw_ref[...] = jnp.concatenate([1.0 / den, e2 / den], axis=1)


def _router(x, g_in, router_w, router_b):
    n, d = x.shape
    ne = router_w.shape[1]
    rw = jnp.zeros((d, LANES), BF16).at[:, :ne].set(router_w.astype(BF16))
    rb = jnp.full((1, LANES), NEG, F32).at[0, :ne].set(router_b.astype(F32))
    return pl.pallas_call(
        _router_kernel,
        grid=(n // TM,),
        in_specs=[pl.BlockSpec((TM, d), lambda i: (i, 0)),
                  pl.BlockSpec((1, d), lambda i: (0, 0)),
                  pl.BlockSpec((d, LANES), lambda i: (0, 0)),
                  pl.BlockSpec((1, LANES), lambda i: (0, 0))],
        out_specs=[pl.BlockSpec((TM, 2), lambda i: (i, 0)), pl.BlockSpec((TM, 2), lambda i: (i, 0))],
        out_shape=[jax.ShapeDtypeStruct((n, 2), jnp.int32), jax.ShapeDtypeStruct((n, 2), F32)],
        compiler_params=_cparams("parallel"),
        name="moe_router",
    )(x, g_in.reshape(1, d), rw, rb)


def _route(topi, ne, t):
    n = topi.shape[0]
    p = 2 * n
    e = topi.reshape(p)
    experts = jnp.arange(ne, dtype=jnp.int32)
    onehot = (e[:, None] == experts[None, :]).astype(jnp.int32)
    csum = jnp.cumsum(onehot, axis=0)
    rank = jnp.sum(onehot * csum, axis=1) - 1
    cnt = csum[-1]
    padded = ((cnt + t - 1) // t) * t
    ends = jnp.cumsum(padded)
    starts = ends - padded
    dest = jnp.sum(onehot * starts[None, :], axis=1) + rank
    n_tiles = p // t + ne
    tile_expert = jnp.sum(jnp.arange(n_tiles, dtype=jnp.int32)[:, None] * t >= ends[None, :], axis=1)
    tile_expert = jnp.minimum(tile_expert, ne - 1).astype(jnp.int32)
    _, tok_sorted = lax.sort_key_val(dest.astype(jnp.int32), jnp.arange(p, dtype=jnp.int32) // 2)
    pad_before = starts - (jnp.cumsum(cnt) - cnt)
    row_expert = jnp.repeat(tile_expert, t)
    row_shift = jnp.sum((row_expert[:, None] == experts[None, :]) * pad_before[None, :], axis=1)
    src = jnp.take(tok_sorted, jnp.clip(jnp.arange(n_tiles * t, dtype=jnp.int32) - row_shift, 0, p - 1))
    dest_tiles = dest.reshape(n // TM, TM, 2).transpose(0, 2, 1).reshape(n // TM, 2 * TM)
    return src.reshape(n_tiles, t), tile_expert, dest_tiles.astype(jnp.int32)


def _row_copy(src_hbm, row, dst_ref, r, sem):
    return pltpu.make_async_copy(src_hbm.at[pl.ds(row, 1)], dst_ref.at[pl.ds(r, 1)], sem)


def _start_row_gather(idx_ref, islot, base, n_rows, src_hbm, dst_ref, sem):
    def body(r, c):
        _row_copy(src_hbm, idx_ref[islot, base + r], dst_ref, r, sem).start()
        return c
    lax.fori_loop(0, n_rows, body, 0, unroll=8)


def _wait_row_gather(n_rows, src_hbm, dst_ref, sem):
    def body(r, c):
        _row_copy(src_hbm, 0, dst_ref, 0, sem).wait()
        return c
    lax.fori_loop(0, n_rows, body, 0, unroll=8)


def _prefetch_schedule(t, n_steps, idx_hbm, idx_ref, isem, start_gather):
    def idx_copy(step):
        return pltpu.make_async_copy(idx_hbm.at[step], idx_ref.at[step % 3], isem.at[step % 3])

    @pl.when(t == 0)
    def _():
        idx_copy(0).start()
        idx_copy(0).wait()
        start_gather(0)
        if n_steps > 1:
            idx_copy(1).start()

    @pl.when(t + 1 < n_steps)
    def _():
        idx_copy(t + 1).wait()

        @pl.when(t + 2 < n_steps)
        def _():
            idx_copy(t + 2).start()

        start_gather(t + 1)


def _moe_expert_kernel(te_ref, src_hbm, x_hbm, gi_ref, wg_ref, wu_ref, wd_ref, o_ref,
                       idx_ref, xg_ref, isem, gsem, h_ref, acc_ref):
    t = pl.program_id(0)
    j = pl.program_id(1)
    rows = xg_ref.shape[1]

    @pl.when(j == 0)
    def _():
        def start_gather(step):
            _start_row_gather(idx_ref, step % 3, 0, rows, x_hbm, xg_ref.at[step % 2], gsem.at[step % 2])

        _prefetch_schedule(t, src_hbm.shape[0], src_hbm, idx_ref, isem, start_gather)
        _wait_row_gather(rows, x_hbm, xg_ref.at[t % 2], gsem.at[t % 2])
        h_ref[...] = _rms(xg_ref[t % 2], gi_ref[...]).astype(BF16)
        acc_ref[...] = jnp.zeros_like(acc_ref)

    h = h_ref[...]
    a = jnp.dot(h, wg_ref[...], preferred_element_type=F32)
    b = jnp.dot(h, wu_ref[...], preferred_element_type=F32)
    mid = (a * jax.nn.sigmoid(a) * b).astype(BF16)
    acc_ref[...] += jnp.dot(mid, wd_ref[...], preferred_element_type=F32)

    @pl.when(j == pl.num_programs(1) - 1)
    def _():
        o_ref[...] = acc_ref[...]


def _moe_experts(x, g_in, src, tile_expert, wg, wu, wd):
    n, d = x.shape
    n_tiles, t = src.shape
    nf = wg.shape[2] // TF
    return pl.pallas_call(
        _moe_expert_kernel,
        grid_spec=pltpu.PrefetchScalarGridSpec(
            num_scalar_prefetch=1,
            grid=(n_tiles, nf),
            in_specs=[pl.BlockSpec(memory_space=pl.ANY),
                      pl.BlockSpec(memory_space=pl.ANY),
                      pl.BlockSpec((1, d), lambda i, j, te: (0, 0)),
                      pl.BlockSpec((None, d, TF), lambda i, j, te: (te[i], 0, j)),
                      pl.BlockSpec((None, d, TF), lambda i, j, te: (te[i], 0, j)),
                      pl.BlockSpec((None, TF, d), lambda i, j, te: (te[i], j, 0))],
            out_specs=pl.BlockSpec((t, d), lambda i, j, te: (i, 0)),
            scratch_shapes=[pltpu.SMEM((3, t), jnp.int32),
                            pltpu.VMEM((2, t, d), F32),
                            pltpu.SemaphoreType.DMA((3,)),
                            pltpu.SemaphoreType.DMA((2,)),
                            pltpu.VMEM((t, d), BF16),
                            pltpu.VMEM((t, d), F32)]),
        out_shape=jax.ShapeDtypeStruct((n_tiles * t, d), F32),
        compiler_params=_cparams("arbitrary", "arbitrary"),
        name="moe_experts",
    )(tile_expert, src, x, g_in.reshape(1, d), wg, wu, wd)


def _moe_combine_kernel(dest_hbm, y_hbm, tw_ref, x_ref, go_ref, o_ref, idx_ref, buf_ref, isem, gsem):
    i = pl.program_id(0)
    tm = x_ref.shape[0]

    def start_gather(step):
        for slot in range(2):
            _start_row_gather(idx_ref, step % 3, slot * tm, tm, y_hbm,
                              buf_ref.at[step % 2, slot], gsem.at[step % 2])

    _prefetch_schedule(i, dest_hbm.shape[0], dest_hbm, idx_ref, isem, start_gather)
    for slot in range(2):
        _wait_row_gather(tm, y_hbm, buf_ref.at[i % 2, slot], gsem.at[i % 2])
    tw = tw_ref[...]
    y = tw[:, 0:1] * buf_ref[i % 2, 0] + tw[:, 1:2] * buf_ref[i % 2, 1]
    o_ref[...] = x_ref[...] + _rms(y, go_ref[...])


def _moe_combine(y, dest_tiles, topw, x, g_out):
    n, d = x.shape
    return pl.pallas_call(
        _moe_combine_kernel,
        grid=(n // TM,),
        in_specs=[pl.BlockSpec(memory_space=pl.ANY),
                  pl.BlockSpec(memory_space=pl.ANY),
                  pl.BlockSpec((TM, 2), lambda i: (i, 0)),
                  pl.BlockSpec((TM, d), lambda i: (i, 0)),
                  pl.BlockSpec((1, d), lambda i: (0, 0))],
        out_specs=pl.BlockSpec((TM, d), lambda i: (i, 0)),
        out_shape=jax.ShapeDtypeStruct((n, d), F32),
        scratch_shapes=[pltpu.SMEM((3, 2 * TM), jnp.int32),
                        pltpu.VMEM((2, 2, TM, d), F32),
                        pltpu.SemaphoreType.DMA((3,)),
                        pltpu.SemaphoreType.DMA((2,))],
        compiler_params=_cparams("arbitrary"),
        name="moe_combine",
    )(dest_tiles, y, topw, x, g_out.reshape(1, d))


def _moe(x, g_in, router_w, router_b, wg, wu, wd, g_out):
    topi, topw = _router(x, g_in, router_w, router_b)
    src, tile_expert, dest_tiles = _route(topi, wg.shape[0], TM)
    y = _moe_experts(x, g_in, src, tile_expert, wg, wu, wd)
    return _moe_combine(y, dest_tiles, topw, x, g_out)


def _trunk(x, seq_len, p):
    n = x.shape[0]
    depth = p["norm_mix"].shape[0]
    for layer in range(depth):
        j = layer // 2
        if layer % 2 == 0:
            o_q = POOL_WIDTH
            o_kv = o_q + WIN_Q_HEADS * HEAD_DIM
            o_end = o_kv + 2 * WIN_KV_HEADS * HEAD_DIM
            u, q, kv = _norm_matmul(x, p["norm_mix"][layer, 0], p["w_in0"][j],
                                    ((0, o_q), (o_q, o_kv), (o_kv, o_end)), (F32, BF16, BF16))
            a = _pool_mixer(u, p["pool_w"][j], p["pool_scale"][j], seq_len)
            b = _window_attention(q, kv, p["win_bias"], p["sink"][j], seq_len)
            w_out = p["w_out0"][j]
            x = _proj_norm_res([a, b], [w_out[:POOL_WIDTH], w_out[POOL_WIDTH:]], x, p["norm_mix"][layer, 1])
            x = _ffn(x, p["norm_ffn"][layer, 0], p["ffn_wg"][j], p["ffn_wu"][j], p["ffn_wd"][j],
                     p["norm_ffn"][layer, 1])
        else:
            lambda_init = 0.8 - 0.6 * math.exp(-0.3 * layer)
            dq = DIFF_HEADS * 2 * HEAD_DIM
            q, k, v = _norm_matmul(x, p["norm_mix"][layer, 0], p["w_in1"][j],
                                   ((0, dq), (dq, 2 * dq), (2 * dq, 3 * dq)), (BF16, BF16, BF16))
            qt = q.T
            vt = v.reshape(n // TK_DIFF, TK_DIFF, dq).transpose(0, 2, 1)
            m = _diff_attention(qt, k, vt, p["diff_bias"], p["diff_lambda"][j], p["subln_g"][j],
                                seq_len, lambda_init)
            x = _proj_norm_res([m], [p["w_out1"][j]], x, p["norm_mix"][layer, 1])
            x = _moe(x, p["norm_ffn"][layer, 0], p["router_w"][j], p["router_b"][j],
                     p["moe_wg"][j], p["moe_wu"][j], p["moe_wd"][j], p["norm_ffn"][layer, 1])
    return x


def _prepare(norm_mix, norm_ffn, rel_bias, w_in0, pool_w, pool_scale, sink, w_out0,
             ffn_wg, ffn_wu, ffn_wd, w_in1, diff_lambda, subln_g, w_out1,
             router_w, router_b, moe_wg, moe_wu, moe_wd):
    scale = HEAD_DIM ** -0.5
    o_q = POOL_WIDTH
    o_kv = o_q + WIN_Q_HEADS * HEAD_DIM
    col0 = jnp.arange(w_in0.shape[-1])
    s0 = jnp.where((col0 >= o_q) & (col0 < o_kv), scale, 1.0).astype(F32)
    col1 = jnp.arange(w_in1.shape[-1])
    s1 = jnp.where(col1 < DIFF_HEADS * 2 * HEAD_DIM, scale * LOG2E, 1.0).astype(F32)
    return dict(
        norm_mix=norm_mix.astype(F32), norm_ffn=norm_ffn.astype(F32),
        win_bias=_window_bias(rel_bias, TQ_WIN), diff_bias=_diff_bias(rel_bias),
        w_in0=(w_in0 * s0).astype(BF16), pool_w=pool_w.astype(BF16), pool_scale=pool_scale.astype(F32),
        sink=sink, w_out0=w_out0.astype(BF16),
        ffn_wg=ffn_wg.astype(BF16), ffn_wu=ffn_wu.astype(BF16), ffn_wd=ffn_wd.astype(BF16),
        w_in1=(w_in1 * s1).astype(BF16), diff_lambda=diff_lambda, subln_g=subln_g,
        w_out1=w_out1.astype(BF16), router_w=router_w, router_b=router_b,
        moe_wg=moe_wg.astype(BF16), moe_wu=moe_wu.astype(BF16), moe_wd=moe_wd.astype(BF16))


def kernel(x_prompt, x_sample, norm_mix, norm_ffn, rel_bias, w_in0, pool_w, pool_scale, sink, w_out0,
           ffn_wg, ffn_wu, ffn_wd, w_in1, diff_lambda, subln_g, w_out1,
           router_w, router_b, moe_wg, moe_wu, moe_wd):
    p = _prepare(norm_mix, norm_ffn, rel_bias, w_in0, pool_w, pool_scale, sink, w_out0,
                 ffn_wg, ffn_wu, ffn_wd, w_in1, diff_lambda, subln_g, w_out1,
                 router_w, router_b, moe_wg, moe_wu, moe_wd)
    outs = []
    for x in (x_prompt, x_sample):
        b, s, d = x.shape
        outs.append(_trunk(x.reshape(b * s, d), s, p).reshape(b, s, d))
    return tuple(outs)
```

```python
import functools
import math

import jax
import jax.numpy as jnp
from jax import lax
from jax.experimental import pallas as pl
from jax.experimental.pallas import tpu as pltpu

D_MODEL = 1024
HEAD_DIM = 64
POOL_WIDTH = 512
POOL_WINDOWS = (2, 4, 8, 16)
POOL_GW = 128
POOL_HALO = 8
WIN_Q_HEADS = 8
WIN_KV_HEADS = 2
WIN_GROUP = 4
WINDOW = 128
DIFF_HEADS = 8
DIFF_VDIM = 128
NUM_BUCKETS = 32
MAX_DISTANCE = 128
D_FF = 2816
N_EXPERTS = 8
EPS = 1e-6
NEG = -1e30
LOG2E = 1.4426950408889634

LANES = 128
BF16_SUBLANES = 16
VMEM_LIMIT = 56 * 1024 * 1024

TM = 512
TF = 1408
TQ_WIN = 256
TQ_DIFF = 1024
TK_DIFF = 1024
TB_DIFF = 256
DIFF_UNROLL = 2

F32 = jnp.float32
BF16 = jnp.bfloat16


def _cparams(*sem):
    return pltpu.CompilerParams(dimension_semantics=sem, vmem_limit_bytes=VMEM_LIMIT)


def _rms(xf, g):
    ms = jnp.mean(xf * xf, axis=-1, keepdims=True)
    return xf * lax.rsqrt(ms + EPS) * g


def _t5_bucket(rel):
    half = NUM_BUCKETS // 2
    max_exact = half // 2
    ret = jnp.where(rel > 0, half, 0)
    n = jnp.abs(rel)
    nf = jnp.maximum(n, 1).astype(jnp.float32)
    large = max_exact + (jnp.log(nf / max_exact) / math.log(MAX_DISTANCE / max_exact)
                         * (half - max_exact)).astype(jnp.int32)
    large = jnp.minimum(large, half - 1)
    return ret + jnp.where(n < max_exact, n, large)


def _bias_lookup(rel_bias, rel):
    onehot = (_t5_bucket(rel)[..., None] == jnp.arange(NUM_BUCKETS)).astype(F32)
    out = jnp.einsum("...b,bh->h...", onehot, rel_bias.astype(F32), precision=lax.Precision.HIGHEST)
    return out


def _norm_matmul_kernel(x_ref, g_ref, w_ref, *o_refs, splits):
    h = _rms(x_ref[...], g_ref[...]).astype(BF16)
    z = jnp.dot(h, w_ref[...], preferred_element_type=F32)
    for o_ref, (a, b) in zip(o_refs, splits):
        o_ref[...] = z[:, a:b].astype(o_ref.dtype)


def _norm_matmul(x, g, w, splits, dtypes):
    n, d = x.shape
    nout = w.shape[1]
    return pl.pallas_call(
        functools.partial(_norm_matmul_kernel, splits=splits),
        grid=(n // TM,),
        in_specs=[pl.BlockSpec((TM, d), lambda i: (i, 0)),
                  pl.BlockSpec((1, d), lambda i: (0, 0)),
                  pl.BlockSpec((d, nout), lambda i: (0, 0))],
        out_specs=[pl.BlockSpec((TM, b - a), lambda i: (i, 0)) for a, b in splits],
        out_shape=[jax.ShapeDtypeStruct((n, b - a), dt) for (a, b), dt in zip(splits, dtypes)],
        compiler_params=_cparams("parallel"),
        name="norm_matmul",
    )(x, g.reshape(1, d), w)


def _proj_norm_res_kernel(*refs, n_in):
    ins, ws = refs[:n_in], refs[n_in:2 * n_in]
    x_ref, g_ref, o_ref = refs[2 * n_in:]
    m = jnp.dot(ins[0][...], ws[0][...], preferred_element_type=F32)
    for a_ref, w_ref in zip(ins[1:], ws[1:]):
        m = m + jnp.dot(a_ref[...], w_ref[...], preferred_element_type=F32)
    o_ref[...] = x_ref[...] + _rms(m, g_ref[...])


def _proj_norm_res(ins, ws, x, g):
    n, d = x.shape
    n_in = len(ins)
    return pl.pallas_call(
        functools.partial(_proj_norm_res_kernel, n_in=n_in),
        grid=(n // TM,),
        in_specs=([pl.BlockSpec((TM, a.shape[1]), lambda i: (i, 0)) for a in ins]
                  + [pl.BlockSpec(w.shape, lambda i: (0, 0)) for w in ws]
                  + [pl.BlockSpec((TM, d), lambda i: (i, 0)),
                     pl.BlockSpec((1, d), lambda i: (0, 0))]),
        out_specs=pl.BlockSpec((TM, d), lambda i: (i, 0)),
        out_shape=jax.ShapeDtypeStruct((n, d), F32),
        compiler_params=_cparams("parallel"),
        name="proj_norm_res",
    )(*ins, *ws, x, g.reshape(1, d))


def _pool_kernel(up_ref, u_ref, un_ref, w_ref, sc_ref, o_ref, ext_ref, *, seq_len):
    tm = u_ref.shape[0]
    t0 = (pl.program_id(0) * tm) % seq_len
    ext_ref[0:POOL_HALO, :] = jnp.where(t0 > 0, up_ref[...], 0.0)
    ext_ref[POOL_HALO:POOL_HALO + tm, :] = u_ref[...]
    ext_ref[POOL_HALO + tm:, :] = jnp.where(t0 + tm < seq_len, un_ref[...], 0.0)
    t = t0 + lax.broadcasted_iota(jnp.int32, (tm, 1), 0)
    outs = []
    for g, w in enumerate(POOL_WINDOWS):
        half = w // 2
        sl = slice(g * POOL_GW, (g + 1) * POOL_GW)
        acc = ext_ref[POOL_HALO - half:POOL_HALO - half + tm, sl]
        for j in range(-half + 1, half):
            acc = acc + ext_ref[POOL_HALO + j:POOL_HALO + j + tm, sl]
        lo = jnp.maximum(t - half, 0)
        hi = jnp.minimum(t + half - 1, seq_len - 1)
        cnt = (hi - lo + 1).astype(F32)
        d = (acc / cnt - u_ref[:, sl]).astype(BF16)
        outs.append(jnp.dot(d, w_ref[g], preferred_element_type=F32))
    o_ref[...] = (jnp.concatenate(outs, axis=-1) * sc_ref[...]).astype(o_ref.dtype)


def _pool_mixer(u, pool_w, pool_scale, seq_len):
    n, c = u.shape
    hb = TM // POOL_HALO
    last = n // POOL_HALO - 1
    return pl.pallas_call(
        functools.partial(_pool_kernel, seq_len=seq_len),
        grid=(n // TM,),
        in_specs=[pl.BlockSpec((POOL_HALO, c), lambda i: (jnp.maximum(i * hb - 1, 0), 0)),
                  pl.BlockSpec((TM, c), lambda i: (i, 0)),
                  pl.BlockSpec((POOL_HALO, c), lambda i: (jnp.minimum((i + 1) * hb, last), 0)),
                  pl.BlockSpec(pool_w.shape, lambda i: (0, 0, 0)),
                  pl.BlockSpec((1, c), lambda i: (0, 0))],
        out_specs=pl.BlockSpec((TM, c), lambda i: (i, 0)),
        out_shape=jax.ShapeDtypeStruct((n, c), BF16),
        scratch_shapes=[pltpu.VMEM((TM + 2 * POOL_HALO, c), F32)],
        compiler_params=_cparams("parallel"),
        name="pool_mixer",
    )(u, u, u, pool_w, pool_scale.reshape(1, c))


def _win_kernel(q_ref, kvp_ref, kv_ref, kvn_ref, bias_ref, sink_ref, o_ref, *, seq_len):
    tq = q_ref.shape[0]
    nk = tq + 2 * WINDOW
    t0 = (pl.program_id(0) * tq) % seq_len
    kv = jnp.concatenate([kvp_ref[...], kv_ref[...], kvn_ref[...]], axis=0)
    kpos = t0 - WINDOW + lax.broadcasted_iota(jnp.int32, (1, nk), 1)
    valid = (kpos >= 0) & (kpos < seq_len)
    q = q_ref[...]
    sink = sink_ref[...]
    outs = []
    for g in range(WIN_KV_HEADS):
        k = kv[:, g * HEAD_DIM:(g + 1) * HEAD_DIM]
        v = kv[:, (WIN_KV_HEADS + g) * HEAD_DIM:(WIN_KV_HEADS + g + 1) * HEAD_DIM]
        for r in range(WIN_GROUP):
            h = g * WIN_GROUP + r
            qh = q[:, h * HEAD_DIM:(h + 1) * HEAD_DIM]
            s = lax.dot_general(qh, k, (((1,), (1,)), ((), ())), preferred_element_type=F32)
            s = jnp.where(valid, s + bias_ref[h], NEG)
            sk = sink[:, h:h + 1]
            m = jnp.maximum(jnp.max(s, axis=-1, keepdims=True), sk)
            e = jnp.exp(s - m)
            l = jnp.sum(e, axis=-1, keepdims=True) + jnp.exp(sk - m)
            p = (e / l).astype(BF16)
            outs.append(jnp.dot(p, v, preferred_element_type=F32))
    o_ref[...] = jnp.concatenate(outs, axis=-1).astype(o_ref.dtype)


def _window_bias(rel_bias, tq):
    nk = tq + 2 * WINDOW
    rel = jnp.arange(nk)[None, :] - WINDOW - jnp.arange(tq)[:, None]
    return jnp.where((jnp.abs(rel) <= WINDOW)[None], _bias_lookup(rel_bias, rel), NEG)


def _window_attention(q, kv, bias, sink, seq_len):
    n = q.shape[0]
    tq = TQ_WIN
    hb = tq // WINDOW
    last = n // WINDOW - 1
    return pl.pallas_call(
        functools.partial(_win_kernel, seq_len=seq_len),
        grid=(n // tq,),
        in_specs=[pl.BlockSpec((tq, q.shape[1]), lambda i: (i, 0)),
                  pl.BlockSpec((WINDOW, kv.shape[1]), lambda i: (jnp.maximum(i * hb - 1, 0), 0)),
                  pl.BlockSpec((tq, kv.shape[1]), lambda i: (i, 0)),
                  pl.BlockSpec((WINDOW, kv.shape[1]), lambda i: (jnp.minimum((i + 1) * hb, last), 0)),
                  pl.BlockSpec(bias.shape, lambda i: (0, 0, 0)),
                  pl.BlockSpec((1, WIN_Q_HEADS), lambda i: (0, 0))],
        out_specs=pl.BlockSpec((tq, q.shape[1]), lambda i: (i, 0)),
        out_shape=jax.ShapeDtypeStruct(q.shape, BF16),
        compiler_params=_cparams("parallel"),
        name="window_attention",
    )(q, kv, kv, kv, bias, sink.reshape(1, WIN_Q_HEADS).astype(F32))


def _ffn_kernel(x_ref, gi_ref, wg_ref, wu_ref, wd_ref, go_ref, o_ref, h_ref, acc_ref):
    j = pl.program_id(1)

    @pl.when(j == 0)
    def _():
        h_ref[...] = _rms(x_ref[...], gi_ref[...]).astype(BF16)
        acc_ref[...] = jnp.zeros_like(acc_ref)

    h = h_ref[...]
    a = jnp.dot(h, wg_ref[...], preferred_element_type=F32)
    b = jnp.dot(h, wu_ref[...], preferred_element_type=F32)
    mid = (a * jax.nn.sigmoid(a) * b).astype(BF16)
    acc_ref[...] += jnp.dot(mid, wd_ref[...], preferred_element_type=F32)

    @pl.when(j == pl.num_programs(1) - 1)
    def _():
        o_ref[...] = x_ref[...] + _rms(acc_ref[...], go_ref[...])


def _ffn(x, g_in, wg, wu, wd, g_out):
    n, d = x.shape
    nf = wg.shape[1] // TF
    return pl.pallas_call(
        _ffn_kernel,
        grid=(n // TM, nf),
        in_specs=[pl.BlockSpec((TM, d), lambda i, j: (i, 0)),
                  pl.BlockSpec((1, d), lambda i, j: (0, 0)),
                  pl.BlockSpec((d, TF), lambda i, j: (0, j)),
                  pl.BlockSpec((d, TF), lambda i, j: (0, j)),
                  pl.BlockSpec((TF, d), lambda i, j: (j, 0)),
                  pl.BlockSpec((1, d), lambda i, j: (0, 0))],
        out_specs=pl.BlockSpec((TM, d), lambda i, j: (i, 0)),
        out_shape=jax.ShapeDtypeStruct((n, d), F32),
        scratch_shapes=[pltpu.VMEM((TM, d), BF16), pltpu.VMEM((TM, d), F32)],
        compiler_params=_cparams("parallel", "arbitrary"),
        name="ffn",
    )(x, g_in.reshape(1, d), wg, wu, wd, g_out.reshape(1, d))


def _diff_kernel(qt_ref, k_ref, vt_ref, bias_ref, lam_ref, g_ref, o_ref,
                 qbd_ref, s0_ref, s1_ref, e0_ref, e1_ref, acc_ref, *, lambda_init):
    tq = qt_ref.shape[1]
    nk, _, tk = vt_ref.shape
    i = pl.program_id(2)
    qt = qt_ref[...]
    zero = jnp.zeros((HEAD_DIM, tq), BF16)
    qbd_ref[0:HEAD_DIM, 0:tq] = qt[0:HEAD_DIM]
    qbd_ref[0:HEAD_DIM, tq:] = zero
    qbd_ref[HEAD_DIM:, 0:tq] = zero
    qbd_ref[HEAD_DIM:, tq:] = qt[HEAD_DIM:]
    acc_ref[...] = jnp.zeros_like(acc_ref)
    ones = jnp.ones((BF16_SUBLANES, tk), BF16)

    def scores(j, dst_ref):
        kc = k_ref[pl.ds(pl.multiple_of(j * tk, tk), tk), :]
        s = jnp.dot(kc, qbd_ref[...], preferred_element_type=F32).astype(BF16)
        rows = []
        for c in range(tk // TB_DIFF):
            sc = s[c * TB_DIFF:(c + 1) * TB_DIFF]
            cols = []
            for mp in range(2):
                for a in range(tq // TB_DIFF):
                    d = jnp.clip(j * (tk // TB_DIFF) + c - (i * (tq // TB_DIFF) + a), -2, 2) + 2
                    c0 = mp * tq + a * TB_DIFF
                    cols.append(sc[:, c0:c0 + TB_DIFF] + bias_ref[d])
            rows.append(jnp.concatenate(cols, axis=1))
        s = jnp.concatenate(rows, axis=0)
        dst_ref[...] = s
        return jnp.max(s, axis=0, keepdims=True)

    def softmax(src_ref, dst_ref, cmax, m_old):
        m_new = jnp.maximum(m_old, cmax.astype(F32))
        alpha = jnp.exp2(m_old - m_new)
        dst_ref[...] = jnp.exp2(src_ref[...] - m_new.astype(BF16))
        return m_new, alpha

    def accumulate(j, src_ref, alpha):
        vt = jnp.concatenate([vt_ref[j], ones], axis=0)
        acc_ref[...] = alpha * acc_ref[...] + jnp.dot(vt, src_ref[...], preferred_element_type=F32)

    s_refs = (s0_ref, s1_ref)
    e_refs = (e0_ref, e1_ref)

    def step(j, parity, carry):
        m, a_prev, c_cur = carry
        c_next = scores(j + 1, s_refs[1 - parity])
        m, a_cur = softmax(s_refs[parity], e_refs[parity], c_cur, m)
        accumulate(j - 1, e_refs[1 - parity], a_prev)
        return m, a_cur, c_next

    m = jnp.full((1, 2 * tq), NEG, F32)
    c0 = scores(0, s0_ref)
    c1 = scores(1, s1_ref)
    m, a0 = softmax(s0_ref, e0_ref, c0, m)
    carry = (m, a0, c1)

    n_trips = (nk - 2) // DIFF_UNROLL

    def trip(jj, carry):
        for u in range(DIFF_UNROLL):
            carry = step(DIFF_UNROLL * jj + 1 + u, (1 + u) % 2, carry)
        return carry

    carry = lax.fori_loop(0, n_trips, trip, carry)
    for j in range(1 + DIFF_UNROLL * n_trips, nk - 1):
        carry = step(j, j % 2, carry)
    m, a_prev, c_last = carry
    last = (nk - 1) % 2
    m, a_last = softmax(s_refs[last], e_refs[last], c_last, m)
    accumulate(nk - 2, e_refs[1 - last], a_prev)
    accumulate(nk - 1, e_refs[last], a_last)

    acc = acc_ref[...]
    o = acc[:DIFF_VDIM] / acc[DIFF_VDIM:DIFF_VDIM + 1]
    lp = lam_ref[...]
    lam = (jnp.exp(jnp.sum(lp[0:1] * lp[1:2], axis=-1, keepdims=True))
           - jnp.exp(jnp.sum(lp[2:3] * lp[3:4], axis=-1, keepdims=True)) + lambda_init)
    out = o[:, :tq] - lam * o[:, tq:]
    ms = jnp.mean(out * out, axis=0, keepdims=True)
    y = out * lax.rsqrt(ms + EPS) * g_ref[...] * (1.0 - lambda_init)
    o_ref[...] = y.T.astype(o_ref.dtype)


def _diff_bias(rel_bias):
    t = TB_DIFF
    assert t >= MAX_DISTANCE
    kk = jnp.arange(t)[:, None]
    qq = jnp.arange(t)[None, :]
    rel = jnp.stack([d * t + kk - qq for d in range(-2, 3)])
    return (_bias_lookup(rel_bias, rel) * LOG2E).astype(BF16)


def _diff_attention(qt, k, vt, bias, diff_lambda, subln_g, seq_len, lambda_init):
    n = k.shape[0]
    tq, tk = TQ_DIFF, TK_DIFF
    assert seq_len % tk == 0 and (seq_len // tk) % 2 == 0 and seq_len // tk >= 2
    assert tq % TB_DIFF == 0 and tk % TB_DIFF == 0
    nb = n // seq_len
    nq = seq_len // tq
    nk = seq_len // tk
    return pl.pallas_call(
        functools.partial(_diff_kernel, lambda_init=lambda_init),
        grid=(nb, DIFF_HEADS, nq),
        in_specs=[pl.BlockSpec((DIFF_VDIM, tq), lambda b, h, i: (h, b * nq + i)),
                  pl.BlockSpec((seq_len, DIFF_VDIM), lambda b, h, i: (b, h)),
                  pl.BlockSpec((nk, DIFF_VDIM, tk), lambda b, h, i: (b, h, 0)),
                  pl.BlockSpec((None, 5, TB_DIFF, TB_DIFF), lambda b, h, i: (h, 0, 0, 0)),
                  pl.BlockSpec((4, HEAD_DIM), lambda b, h, i: (0, 0)),
                  pl.BlockSpec((DIFF_VDIM, 1), lambda b, h, i: (0, 0))],
        out_specs=pl.BlockSpec((tq, DIFF_VDIM), lambda b, h, i: (b * nq + i, h)),
        out_shape=jax.ShapeDtypeStruct((n, DIFF_HEADS * DIFF_VDIM), BF16),
        scratch_shapes=[pltpu.VMEM((DIFF_VDIM, 2 * tq), BF16),
                        pltpu.VMEM((tk, 2 * tq), BF16),
                        pltpu.VMEM((tk, 2 * tq), BF16),
                        pltpu.VMEM((tk, 2 * tq), BF16),
                        pltpu.VMEM((tk, 2 * tq), BF16),
                        pltpu.VMEM((DIFF_VDIM + BF16_SUBLANES, 2 * tq), F32)],
        compiler_params=_cparams("parallel", "parallel", "arbitrary"),
        name="diff_attention",
    )(qt, k, vt, bias, diff_lambda.astype(F32), subln_g.reshape(DIFF_VDIM, 1).astype(F32))


def _router_kernel(x_ref, gi_ref, rw_ref, rb_ref, ti_ref, tw_ref):
    h = _rms(x_ref[...], gi_ref[...]).astype(BF16)
    logits = jnp.dot(h, rw_ref[...], preferred_element_type=F32) + rb_ref[...]
    col = lax.broadcasted_iota(jnp.int32, logits.shape, 1)
    m1 = jnp.max(logits, axis=-1, keepdims=True)
    i1 = jnp.min(jnp.where(logits == m1, col, LANES), axis=-1, keepdims=True)
    rest = jnp.where(col == i1, 2 * NEG, logits)
    m2 = jnp.max(rest, axis=-1, keepdims=True)
    i2 = jnp.min(jnp.where(rest == m2, col, LANES), axis=-1, keepdims=True)
    e2 = jnp.exp(m2 - m1)
    den = 1.0 + e2
    ti_ref[...] = jnp.concatenate([i1, i2], axis=1)
    tw_ref[...] = jnp.concatenate([1.0 / den, e2 / den], axis=1)


def _router(x, g_in, router_w, router_b):
    n, d = x.shape
    ne = router_w.shape[1]
    rw = jnp.zeros((d, LANES), BF16).at[:, :ne].set(router_w.astype(BF16))
    rb = jnp.full((1, LANES), NEG, F32).at[0, :ne].set(router_b.astype(F32))
    return pl.pallas_call(
        _router_kernel,
        grid=(n // TM,),
        in_specs=[pl.BlockSpec((TM, d), lambda i: (i, 0)),
                  pl.BlockSpec((1, d), lambda i: (0, 0)),
                  pl.BlockSpec((d, LANES), lambda i: (0, 0)),
                  pl.BlockSpec((1, LANES), lambda i: (0, 0))],
        out_specs=[pl.BlockSpec((TM, 2), lambda i: (i, 0)), pl.BlockSpec((TM, 2), lambda i: (i, 0))],
        out_shape=[jax.ShapeDtypeStruct((n, 2), jnp.int32), jax.ShapeDtypeStruct((n, 2), F32)],
        compiler_params=_cparams("parallel"),
        name="moe_router",
    )(x, g_in.reshape(1, d), rw, rb)


def _route(topi, ne, t):
    n = topi.shape[0]
    p = 2 * n
    e = topi.reshape(p)
    experts = jnp.arange(ne, dtype=jnp.int32)
    onehot = (e[:, None] == experts[None, :]).astype(jnp.int32)
    csum = jnp.cumsum(onehot, axis=0)
    rank = jnp.sum(onehot * csum, axis=1) - 1
    cnt = csum[-1]
    padded = ((cnt + t - 1) // t) * t
    ends = jnp.cumsum(padded)
    starts = ends - padded
    dest = jnp.sum(onehot * starts[None, :], axis=1) + rank
    n_tiles = p // t + ne
    tile_expert = jnp.sum(jnp.arange(n_tiles, dtype=jnp.int32)[:, None] * t >= ends[None, :], axis=1)
    tile_expert = jnp.minimum(tile_expert, ne - 1).astype(jnp.int32)
    _, tok_sorted = lax.sort_key_val(dest.astype(jnp.int32), jnp.arange(p, dtype=jnp.int32) // 2)
    pad_before = starts - (jnp.cumsum(cnt) - cnt)
    row_expert = jnp.repeat(tile_expert, t)
    row_shift = jnp.sum((row_expert[:, None] == experts[None, :]) * pad_before[None, :], axis=1)
    src = jnp.take(tok_sorted, jnp.clip(jnp.arange(n_tiles * t, dtype=jnp.int32) - row_shift, 0, p - 1))
    dest_tiles = dest.reshape(n // TM, TM, 2).transpose(0, 2, 1).reshape(n // TM, 2 * TM)
    return src.reshape(n_tiles, t), tile_expert, dest_tiles.astype(jnp.int32)


def _row_copy(src_hbm, row, dst_ref, r, sem):
    return pltpu.make_async_copy(src_hbm.at[pl.ds(row, 1)], dst_ref.at[pl.ds(r, 1)], sem)


def _start_row_gather(idx_ref, islot, base, n_rows, src_hbm, dst_ref, sem):
    def body(r, c):
        _row_copy(src_hbm, idx_ref[islot, base + r], dst_ref, r, sem).start()
        return c
    lax.fori_loop(0, n_rows, body, 0, unroll=8)


def _wait_row_gather(n_rows, src_hbm, dst_ref, sem):
    def body(r, c):
        _row_copy(src_hbm, 0, dst_ref, 0, sem).wait()
        return c
    lax.fori_loop(0, n_rows, body, 0, unroll=8)


def _prefetch_schedule(t, n_steps, idx_hbm, idx_ref, isem, start_gather):
    def idx_copy(step):
        return pltpu.make_async_copy(idx_hbm.at[step], idx_ref.at[step % 3], isem.at[step % 3])

    @pl.when(t == 0)
    def _():
        idx_copy(0).start()
        idx_copy(0).wait()
        start_gather(0)
        if n_steps > 1:
            idx_copy(1).start()

    @pl.when(t + 1 < n_steps)
    def _():
        idx_copy(t + 1).wait()

        @pl.when(t + 2 < n_steps)
        def _():
            idx_copy(t + 2).start()

        start_gather(t + 1)


def _moe_expert_kernel(te_ref, src_hbm, x_hbm, gi_ref, wg_ref, wu_ref, wd_ref, o_ref,
                       idx_ref, xg_ref, isem, gsem, h_ref, acc_ref):
    t = pl.program_id(0)
    j = pl.program_id(1)
    rows = xg_ref.shape[1]

    @pl.when(j == 0)
    def _():
        def start_gather(step):
            _start_row_gather(idx_ref, step % 3, 0, rows, x_hbm, xg_ref.at[step % 2], gsem.at[step % 2])

        _prefetch_schedule(t, src_hbm.shape[0], src_hbm, idx_ref, isem, start_gather)
        _wait_row_gather(rows, x_hbm, xg_ref.at[t % 2], gsem.at[t % 2])
        h_ref[...] = _rms(xg_ref[t % 2], gi_ref[...]).astype(BF16)
        acc_ref[...] = jnp.zeros_like(acc_ref)

    h = h_ref[...]
    a = jnp.dot(h, wg_ref[...], preferred_element_type=F32)
    b = jnp.dot(h, wu_ref[...], preferred_element_type=F32)
    mid = (a * jax.nn.sigmoid(a) * b).astype(BF16)
    acc_ref[...] += jnp.dot(mid, wd_ref[...], preferred_element_type=F32)

    @pl.when(j == pl.num_programs(1) - 1)
    def _():
        o_ref[...] = acc_ref[...]


def _moe_experts(x, g_in, src, tile_expert, wg, wu, wd):
    n, d = x.shape
    n_tiles, t = src.shape
    nf = wg.shape[2] // TF
    return pl.pallas_call(
        _moe_expert_kernel,
        grid_spec=pltpu.PrefetchScalarGridSpec(
            num_scalar_prefetch=1,
            grid=(n_tiles, nf),
            in_specs=[pl.BlockSpec(memory_space=pl.ANY),
                      pl.BlockSpec(memory_space=pl.ANY),
                      pl.BlockSpec((1, d), lambda i, j, te: (0, 0)),
                      pl.BlockSpec((None, d, TF), lambda i, j, te: (te[i], 0, j)),
                      pl.BlockSpec((None, d, TF), lambda i, j, te: (te[i], 0, j)),
                      pl.BlockSpec((None, TF, d), lambda i, j, te: (te[i], j, 0))],
            out_specs=pl.BlockSpec((t, d), lambda i, j, te: (i, 0)),
            scratch_shapes=[pltpu.SMEM((3, t), jnp.int32),
                            pltpu.VMEM((2, t, d), F32),
                            pltpu.SemaphoreType.DMA((3,)),
                            pltpu.SemaphoreType.DMA((2,)),
                            pltpu.VMEM((t, d), BF16),
                            pltpu.VMEM((t, d), F32)]),
        out_shape=jax.ShapeDtypeStruct((n_tiles * t, d), F32),
        compiler_params=_cparams("arbitrary", "arbitrary"),
        name="moe_experts",
    )(tile_expert, src, x, g_in.reshape(1, d), wg, wu, wd)


def _moe_combine_kernel(dest_hbm, y_hbm, tw_ref, x_ref, go_ref, o_ref, idx_ref, buf_ref, isem, gsem):
    i = pl.program_id(0)
    tm = x_ref.shape[0]

    def start_gather(step):
        for slot in range(2):
            _start_row_gather(idx_ref, step % 3, slot * tm, tm, y_hbm,
                              buf_ref.at[step % 2, slot], gsem.at[step % 2])

    _prefetch_schedule(i, dest_hbm.shape[0], dest_hbm, idx_ref, isem, start_gather)
    for slot in range(2):
        _wait_row_gather(tm, y_hbm, buf_ref.at[i % 2, slot], gsem.at[i % 2])
    tw = tw_ref[...]
    y = tw[:, 0:1] * buf_ref[i % 2, 0] + tw[:, 1:2] * buf_ref[i % 2, 1]
    o_ref[...] = x_ref[...] + _rms(y, go_ref[...])


def _moe_combine(y, dest_tiles, topw, x, g_out):
    n, d = x.shape
    return pl.pallas_call(
        _moe_combine_kernel,
        grid=(n // TM,),
        in_specs=[pl.BlockSpec(memory_space=pl.ANY),
                  pl.BlockSpec(memory_space=pl.ANY),
                  pl.BlockSpec((TM, 2), lambda i: (i, 0)),
                  pl.BlockSpec((TM, d), lambda i: (i, 0)),
                  pl.BlockSpec((1, d), lambda i: (0, 0))],
        out_specs=pl.BlockSpec((TM, d), lambda i: (i, 0)),
        out_shape=jax.ShapeDtypeStruct((n, d), F32),
        scratch_shapes=[pltpu.SMEM((3, 2 * TM), jnp.int32),
                        pltpu.VMEM((2, 2, TM, d), F32),
                        pltpu.SemaphoreType.DMA((3,)),
                        pltpu.SemaphoreType.DMA((2,))],
        compiler_params=_cparams("arbitrary"),
        name="moe_combine",
    )(dest_tiles, y, topw, x, g_out.reshape(1, d))


def _moe(x, g_in, router_w, router_b, wg, wu, wd, g_out):
    topi, topw = _router(x, g_in, router_w, router_b)
    src, tile_expert, dest_tiles = _route(topi, wg.shape[0], TM)
    y = _moe_experts(x, g_in, src, tile_expert, wg, wu, wd)
    return _moe_combine(y, dest_tiles, topw, x, g_out)


def _trunk(x, seq_len, p):
    n = x.shape[0]
    depth = p["norm_mix"].shape[0]
    for layer in range(depth):
        j = layer // 2
        if layer % 2 == 0:
            o_q = POOL_WIDTH
            o_kv = o_q + WIN_Q_HEADS * HEAD_DIM
            o_end = o_kv + 2 * WIN_KV_HEADS * HEAD_DIM
            u, q, kv = _norm_matmul(x, p["norm_mix"][layer, 0], p["w_in0"][j],
                                    ((0, o_q), (o_q, o_kv), (o_kv, o_end)), (F32, BF16, BF16))
            a = _pool_mixer(u, p["pool_w"][j], p["pool_scale"][j], seq_len)
            b = _window_attention(q, kv, p["win_bias"], p["sink"][j], seq_len)
            w_out = p["w_out0"][j]
            x = _proj_norm_res([a, b], [w_out[:POOL_WIDTH], w_out[POOL_WIDTH:]], x, p["norm_mix"][layer, 1])
            x = _ffn(x, p["norm_ffn"][layer, 0], p["ffn_wg"][j], p["ffn_wu"][j], p["ffn_wd"][j],
                     p["norm_ffn"][layer, 1])
        else:
            lambda_init = 0.8 - 0.6 * math.exp(-0.3 * layer)
            dq = DIFF_HEADS * 2 * HEAD_DIM
            q, k, v = _norm_matmul(x, p["norm_mix"][layer, 0], p["w_in1"][j],
                                   ((0, dq), (dq, 2 * dq), (2 * dq, 3 * dq)), (BF16, BF16, BF16))
            qt = q.T
            vt = v.reshape(n // TK_DIFF, TK_DIFF, dq).transpose(0, 2, 1)
            m = _diff_attention(qt, k, vt, p["diff_bias"], p["diff_lambda"][j], p["subln_g"][j],
                                seq_len, lambda_init)
            x = _proj_norm_res([m], [p["w_out1"][j]], x, p["norm_mix"][layer, 1])
            x = _moe(x, p["norm_ffn"][layer, 0], p["router_w"][j], p["router_b"][j],
                     p["moe_wg"][j], p["moe_wu"][j], p["moe_wd"][j], p["norm_ffn"][layer, 1])
    return x


def _prepare(norm_mix, norm_ffn, rel_bias, w_in0, pool_w, pool_scale, sink, w_out0,
             ffn_wg, ffn_wu, ffn_wd, w_in1, diff_lambda, subln_g, w_out1,
             router_w, router_b, moe_wg, moe_wu, moe_wd):
    scale = HEAD_DIM ** -0.5
    o_q = POOL_WIDTH
    o_kv = o_q + WIN_Q_HEADS * HEAD_DIM
    col0 = jnp.arange(w_in0.shape[-1])
    s0 = jnp.where((col0 >= o_q) & (col0 < o_kv), scale, 1.0).astype(F32)
    col1 = jnp.arange(w_in1.shape[-1])
    s1 = jnp.where(col1 < DIFF_HEADS * 2 * HEAD_DIM, scale * LOG2E, 1.0).astype(F32)
    return dict(
        norm_mix=norm_mix.astype(F32), norm_ffn=norm_ffn.astype(F32),
        win_bias=_window_bias(rel_bias, TQ_WIN), diff_bias=_diff_bias(rel_bias),
        w_in0=(w_in0 * s0).astype(BF16), pool_w=pool_w.astype(BF16), pool_scale=pool_scale.astype(F32),
        sink=sink, w_out0=w_out0.astype(BF16),
        ffn_wg=ffn_wg.astype(BF16), ffn_wu=ffn_wu.astype(BF16), ffn_wd=ffn_wd.astype(BF16),
        w_in1=(w_in1 * s1).astype(BF16), diff_lambda=diff_lambda, subln_g=subln_g,
        w_out1=w_out1.astype(BF16), router_w=router_w, router_b=router_b,
        moe_wg=moe_wg.astype(BF16), moe_wu=moe_wu.astype(BF16), moe_wd=moe_wd.astype(BF16))


def kernel(x_prompt, x_sample, norm_mix, norm_ffn, rel_bias, w_in0, pool_w, pool_scale, sink, w_out0,
           ffn_wg, ffn_wu, ffn_wd, w_in1, diff_lambda, subln_g, w_out1,
           router_w, router_b, moe_wg, moe_wu, moe_wd):
    p = _prepare(norm_mix, norm_ffn, rel_bias, w_in0, pool_w, pool_scale, sink, w_out0,
                 ffn_wg, ffn_wu, ffn_wd, w_in1, diff_lambda, subln_g, w_out1,
                 router_w, router_b, moe_wg, moe_wu, moe_wd)
    outs = []
    for x in (x_prompt, x_sample):
        b, s, d = x.shape
        outs.append(_trunk(x.reshape(b * s, d), s, p).reshape(b, s, d))
    return tuple(outs)
```

```python
import functools
import math

import jax
import jax.numpy as jnp
from jax import lax
from jax.experimental import pallas as pl
from jax.experimental.pallas import tpu as pltpu

D_MODEL = 1024
HEAD_DIM = 64
POOL_WIDTH = 512
POOL_WINDOWS = (2, 4, 8, 16)
POOL_GW = 128
POOL_HALO = 8
WIN_Q_HEADS = 8
WIN_KV_HEADS = 2
WIN_GROUP = 4
WINDOW = 128
DIFF_HEADS = 8
DIFF_VDIM = 128
NUM_BUCKETS = 32
MAX_DISTANCE = 128
D_FF = 2816
N_EXPERTS = 8
EPS = 1e-6
NEG = -1e30
LOG2E = 1.4426950408889634

LANES = 128
BF16_SUBLANES = 16
VMEM_LIMIT = 56 * 1024 * 1024

TM = 512
TF = 1408
TQ_WIN = 256
TQ_DIFF = 1024
TK_DIFF = 1024
TB_DIFF = 256
DIFF_UNROLL = 2

F32 = jnp.float32
BF16 = jnp.bfloat16


def _cparams(*sem):
    return pltpu.CompilerParams(dimension_semantics=sem, vmem_limit_bytes=VMEM_LIMIT)


def _rms(xf, g):
    ms = jnp.mean(xf * xf, axis=-1, keepdims=True)
    return xf * lax.rsqrt(ms + EPS) * g


def _t5_bucket(rel):
    half = NUM_BUCKETS // 2
    max_exact = half // 2
    ret = jnp.where(rel > 0, half, 0)
    n = jnp.abs(rel)
    nf = jnp.maximum(n, 1).astype(jnp.float32)
    large = max_exact + (jnp.log(nf / max_exact) / math.log(MAX_DISTANCE / max_exact)
                         * (half - max_exact)).astype(jnp.int32)
    large = jnp.minimum(large, half - 1)
    return ret + jnp.where(n < max_exact, n, large)


def _bias_lookup(rel_bias, rel):
    onehot = (_t5_bucket(rel)[..., None] == jnp.arange(NUM_BUCKETS)).astype(F32)
    out = jnp.einsum("...b,bh->h...", onehot, rel_bias.astype(F32), precision=lax.Precision.HIGHEST)
    return out


def _norm_matmul_kernel(x_ref, g_ref, w_ref, *o_refs, splits):
    h = _rms(x_ref[...], g_ref[...]).astype(BF16)
    z = jnp.dot(h, w_ref[...], preferred_element_type=F32)
    for o_ref, (a, b) in zip(o_refs, splits):
        o_ref[...] = z[:, a:b].astype(o_ref.dtype)


def _norm_matmul(x, g, w, splits, dtypes):
    n, d = x.shape
    nout = w.shape[1]
    return pl.pallas_call(
        functools.partial(_norm_matmul_kernel, splits=splits),
        grid=(n // TM,),
        in_specs=[pl.BlockSpec((TM, d), lambda i: (i, 0)),
                  pl.BlockSpec((1, d), lambda i: (0, 0)),
                  pl.BlockSpec((d, nout), lambda i: (0, 0))],
        out_specs=[pl.BlockSpec((TM, b - a), lambda i: (i, 0)) for a, b in splits],
        out_shape=[jax.ShapeDtypeStruct((n, b - a), dt) for (a, b), dt in zip(splits, dtypes)],
        compiler_params=_cparams("parallel"),
        name="norm_matmul",
    )(x, g.reshape(1, d), w)


def _proj_norm_res_kernel(*refs, n_in):
    ins, ws = refs[:n_in], refs[n_in:2 * n_in]
    x_ref, g_ref, o_ref = refs[2 * n_in:]
    m = jnp.dot(ins[0][...], ws[0][...], preferred_element_type=F32)
    for a_ref, w_ref in zip(ins[1:], ws[1:]):
        m = m + jnp.dot(a_ref[...], w_ref[...], preferred_element_type=F32)
    o_ref[...] = x_ref[...] + _rms(m, g_ref[...])


def _proj_norm_res(ins, ws, x, g):
    n, d = x.shape
    n_in = len(ins)
    return pl.pallas_call(
        functools.partial(_proj_norm_res_kernel, n_in=n_in),
        grid=(n // TM,),
        in_specs=([pl.BlockSpec((TM, a.shape[1]), lambda i: (i, 0)) for a in ins]
                  + [pl.BlockSpec(w.shape, lambda i: (0, 0)) for w in ws]
                  + [pl.BlockSpec((TM, d), lambda i: (i, 0)),
                     pl.BlockSpec((1, d), lambda i: (0, 0))]),
        out_specs=pl.BlockSpec((TM, d), lambda i: (i, 0)),
        out_shape=jax.ShapeDtypeStruct((n, d), F32),
        compiler_params=_cparams("parallel"),
        name="proj_norm_res",
    )(*ins, *ws, x, g.reshape(1, d))


def _pool_kernel(up_ref, u_ref, un_ref, w_ref, sc_ref, o_ref, ext_ref, *, seq_len):
    tm = u_ref.shape[0]
    t0 = (pl.program_id(0) * tm) % seq_len
    ext_ref[0:POOL_HALO, :] = jnp.where(t0 > 0, up_ref[...], 0.0)
    ext_ref[POOL_HALO:POOL_HALO + tm, :] = u_ref[...]
    ext_ref[POOL_HALO + tm:, :] = jnp.where(t0 + tm < seq_len, un_ref[...], 0.0)
    t = t0 + lax.broadcasted_iota(jnp.int32, (tm, 1), 0)
    outs = []
    for g, w in enumerate(POOL_WINDOWS):
        half = w // 2
        sl = slice(g * POOL_GW, (g + 1) * POOL_GW)
        acc = ext_ref[POOL_HALO - half:POOL_HALO - half + tm, sl]
        for j in range(-half + 1, half):
            acc = acc + ext_ref[POOL_HALO + j:POOL_HALO + j + tm, sl]
        lo = jnp.maximum(t - half, 0)
        hi = jnp.minimum(t + half - 1, seq_len - 1)
        cnt = (hi - lo + 1).astype(F32)
        d = (acc / cnt - u_ref[:, sl]).astype(BF16)
        outs.append(jnp.dot(d, w_ref[g], preferred_element_type=F32))
    o_ref[...] = (jnp.concatenate(outs, axis=-1) * sc_ref[...]).astype(o_ref.dtype)


def _pool_mixer(u, pool_w, pool_scale, seq_len):
    n, c = u.shape
    hb = TM // POOL_HALO
    last = n // POOL_HALO - 1
    return pl.pallas_call(
        functools.partial(_pool_kernel, seq_len=seq_len),
        grid=(n // TM,),
        in_specs=[pl.BlockSpec((POOL_HALO, c), lambda i: (jnp.maximum(i * hb - 1, 0), 0)),
                  pl.BlockSpec((TM, c), lambda i: (i, 0)),
                  pl.BlockSpec((POOL_HALO, c), lambda i: (jnp.minimum((i + 1) * hb, last), 0)),
                  pl.BlockSpec(pool_w.shape, lambda i: (0, 0, 0)),
                  pl.BlockSpec((1, c), lambda i: (0, 0))],
        out_specs=pl.BlockSpec((TM, c), lambda i: (i, 0)),
        out_shape=jax.ShapeDtypeStruct((n, c), BF16),
        scratch_shapes=[pltpu.VMEM((TM + 2 * POOL_HALO, c), F32)],
        compiler_params=_cparams("parallel"),
        name="pool_mixer",
    )(u, u, u, pool_w, pool_scale.reshape(1, c))


def _win_kernel(q_ref, kvp_ref, kv_ref, kvn_ref, bias_ref, sink_ref, o_ref, *, seq_len):
    tq = q_ref.shape[0]
    nk = tq + 2 * WINDOW
    t0 = (pl.program_id(0) * tq) % seq_len
    kv = jnp.concatenate([kvp_ref[...], kv_ref[...], kvn_ref[...]], axis=0)
    kpos = t0 - WINDOW + lax.broadcasted_iota(jnp.int32, (1, nk), 1)
    valid = (kpos >= 0) & (kpos < seq_len)
    q = q_ref[...]
    sink = sink_ref[...]
    outs = []
    for g in range(WIN_KV_HEADS):
        k = kv[:, g * HEAD_DIM:(g + 1) * HEAD_DIM]
        v = kv[:, (WIN_KV_HEADS + g) * HEAD_DIM:(WIN_KV_HEADS + g + 1) * HEAD_DIM]
        for r in range(WIN_GROUP):
            h = g * WIN_GROUP + r
            qh = q[:, h * HEAD_DIM:(h + 1) * HEAD_DIM]
            s = lax.dot_general(qh, k, (((1,), (1,)), ((), ())), preferred_element_type=F32)
            s = jnp.where(valid, s + bias_ref[h], NEG)
            sk = sink[:, h:h + 1]
            m = jnp.maximum(jnp.max(s, axis=-1, keepdims=True), sk)
            e = jnp.exp(s - m)
            l = jnp.sum(e, axis=-1, keepdims=True) + jnp.exp(sk - m)
            p = (e / l).astype(BF16)
            outs.append(jnp.dot(p, v, preferred_element_type=F32))
    o_ref[...] = jnp.concatenate(outs, axis=-1).astype(o_ref.dtype)


def _window_bias(rel_bias, tq):
    nk = tq + 2 * WINDOW
    rel = jnp.arange(nk)[None, :] - WINDOW - jnp.arange(tq)[:, None]
    return jnp.where((jnp.abs(rel) <= WINDOW)[None], _bias_lookup(rel_bias, rel), NEG)


def _window_attention(q, kv, bias, sink, seq_len):
    n = q.shape[0]
    tq = TQ_WIN
    hb = tq // WINDOW
    last = n // WINDOW - 1
    return pl.pallas_call(
        functools.partial(_win_kernel, seq_len=seq_len),
        grid=(n // tq,),
        in_specs=[pl.BlockSpec((tq, q.shape[1]), lambda i: (i, 0)),
                  pl.BlockSpec((WINDOW, kv.shape[1]), lambda i: (jnp.maximum(i * hb - 1, 0), 0)),
                  pl.BlockSpec((tq, kv.shape[1]), lambda i: (i, 0)),
                  pl.BlockSpec((WINDOW, kv.shape[1]), lambda i: (jnp.minimum((i + 1) * hb, last), 0)),
                  pl.BlockSpec(bias.shape, lambda i: (0, 0, 0)),
                  pl.BlockSpec((1, WIN_Q_HEADS), lambda i: (0, 0))],
        out_specs=pl.BlockSpec((tq, q.shape[1]), lambda i: (i, 0)),
        out_shape=jax.ShapeDtypeStruct(q.shape, BF16),
        compiler_params=_cparams("parallel"),
        name="window_attention",
    )(q, kv, kv, kv, bias, sink.reshape(1, WIN_Q_HEADS).astype(F32))


def _ffn_kernel(x_ref, gi_ref, wg_ref, wu_ref, wd_ref, go_ref, o_ref, h_ref, acc_ref):
    j = pl.program_id(1)

    @pl.when(j == 0)
    def _():
        h_ref[...] = _rms(x_ref[...], gi_ref[...]).astype(BF16)
        acc_ref[...] = jnp.zeros_like(acc_ref)

    h = h_ref[...]
    a = jnp.dot(h, wg_ref[...], preferred_element_type=F32)
    b = jnp.dot(h, wu_ref[...], preferred_element_type=F32)
    mid = (a * jax.nn.sigmoid(a) * b).astype(BF16)
    acc_ref[...] += jnp.dot(mid, wd_ref[...], preferred_element_type=F32)

    @pl.when(j == pl.num_programs(1) - 1)
    def _():
        o_ref[...] = x_ref[...] + _rms(acc_ref[...], go_ref[...])


def _ffn(x, g_in, wg, wu, wd, g_out):
    n, d = x.shape
    nf = wg.shape[1] // TF
    return pl.pallas_call(
        _ffn_kernel,
        grid=(n // TM, nf),
        in_specs=[pl.BlockSpec((TM, d), lambda i, j: (i, 0)),
                  pl.BlockSpec((1, d), lambda i, j: (0, 0)),
                  pl.BlockSpec((d, TF), lambda i, j: (0, j)),
                  pl.BlockSpec((d, TF), lambda i, j: (0, j)),
                  pl.BlockSpec((TF, d), lambda i, j: (j, 0)),
                  pl.BlockSpec((1, d), lambda i, j: (0, 0))],
        out_specs=pl.BlockSpec((TM, d), lambda i, j: (i, 0)),
        out_shape=jax.ShapeDtypeStruct((n, d), F32),
        scratch_shapes=[pltpu.VMEM((TM, d), BF16), pltpu.VMEM((TM, d), F32)],
        compiler_params=_cparams("parallel", "arbitrary"),
        name="ffn",
    )(x, g_in.reshape(1, d), wg, wu, wd, g_out.reshape(1, d))


def _diff_kernel(qt_ref, k_ref, vt_ref, bias_ref, lam_ref, g_ref, o_ref,
                 qbd_ref, s0_ref, s1_ref, e0_ref, e1_ref, acc_ref, *, lambda_init):
    tq = qt_ref.shape[1]
    nk, _, tk = vt_ref.shape
    i = pl.program_id(2)
    qt = qt_ref[...]
    zero = jnp.zeros((HEAD_DIM, tq), BF16)
    qbd_ref[0:HEAD_DIM, 0:tq] = qt[0:HEAD_DIM]
    qbd_ref[0:HEAD_DIM, tq:] = zero
    qbd_ref[HEAD_DIM:, 0:tq] = zero
    qbd_ref[HEAD_DIM:, tq:] = qt[HEAD_DIM:]
    acc_ref[...] = jnp.zeros_like(acc_ref)
    ones = jnp.ones((BF16_SUBLANES, tk), BF16)

    n_near = 3
    base = jnp.clip(i - 1, 0, nk - n_near)
    bias_left = bias_ref[0, 0:1, 0:1].astype(F32)
    bias_right = bias_ref[4, 0:1, 0:1].astype(F32)

    def chunk_at(p):
        far = p - n_near
        return jnp.where(p < n_near, base + p, jnp.where(far < base, far, far + n_near))

    def scores_far(j, dst_ref):
        kc = k_ref[pl.ds(pl.multiple_of(j * tk, tk), tk), :]
        s = jnp.dot(kc, qbd_ref[...], preferred_element_type=F32).astype(BF16)
        dst_ref[...] = s
        off = jnp.where(j < i, bias_left, bias_right)
        return jnp.max(s, axis=0, keepdims=True).astype(F32) + off, off

    def scores(j, dst_ref):
        kc = k_ref[pl.ds(pl.multiple_of(j * tk, tk), tk), :]
        s = jnp.dot(kc, qbd_ref[...], preferred_element_type=F32).astype(BF16)
        rows = []
        for c in range(tk // TB_DIFF):
            sc = s[c * TB_DIFF:(c + 1) * TB_DIFF]
            cols = []
            for mp in range(2):
                for a in range(tq // TB_DIFF):
                    d = jnp.clip(j * (tk // TB_DIFF) + c - (i * (tq // TB_DIFF) + a), -2, 2) + 2
                    c0 = mp * tq + a * TB_DIFF
                    cols.append(sc[:, c0:c0 + TB_DIFF] + bias_ref[d])
            rows.append(jnp.concatenate(cols, axis=1))
        s = jnp.concatenate(rows, axis=0)
        dst_ref[...] = s
        return jnp.max(s, axis=0, keepdims=True).astype(F32), jnp.zeros((1, 1), F32)

    def softmax(src_ref, dst_ref, cmax, off, m_old):
        m_new = jnp.maximum(m_old, cmax)
        alpha = jnp.exp2(m_old - m_new)
        dst_ref[...] = jnp.exp2(src_ref[...] - (m_new - off).astype(BF16))
        return m_new, alpha

    def accumulate(j, src_ref, alpha):
        vt = jnp.concatenate([vt_ref[j], ones], axis=0)
        acc_ref[...] = alpha * acc_ref[...] + jnp.dot(vt, src_ref[...], preferred_element_type=F32)

    s_refs = (s0_ref, s1_ref)
    e_refs = (e0_ref, e1_ref)

    def step(p, parity, next_scores, carry):
        m, a_prev, c_cur, off_cur = carry
        c_next, off_next = next_scores(chunk_at(p + 1), s_refs[1 - parity])
        m, a_cur = softmax(s_refs[parity], e_refs[parity], c_cur, off_cur, m)
        accumulate(chunk_at(p - 1), e_refs[1 - parity], a_prev)
        return m, a_cur, c_next, off_next

    m = jnp.full((1, 2 * tq), NEG, F32)
    c0, off0 = scores(base, s0_ref)
    c1, off1 = scores(base + 1, s1_ref)
    m, a0 = softmax(s0_ref, e0_ref, c0, off0, m)
    carry = step(1, 1, scores, (m, a0, c1, off1))

    n_trips = (nk - n_near) // DIFF_UNROLL

    def trip(jj, carry):
        for u in range(DIFF_UNROLL):
            carry = step(DIFF_UNROLL * jj + 2 + u, u % 2, scores_far, carry)
        return carry

    carry = lax.fori_loop(0, n_trips, trip, carry)
    for p in range(2 + DIFF_UNROLL * n_trips, nk - 1):
        carry = step(p, p % 2, scores_far, carry)
    m, a_prev, c_last, off_last = carry
    last = (nk - 1) % 2
    m, a_last = softmax(s_refs[last], e_refs[last], c_last, off_last, m)
    accumulate(chunk_at(nk - 2), e_refs[1 - last], a_prev)
    accumulate(chunk_at(nk - 1), e_refs[last], a_last)

    acc = acc_ref[...]
    o = acc[:DIFF_VDIM] / acc[DIFF_VDIM:DIFF_VDIM + 1]
    lp = lam_ref[...]
    lam = (jnp.exp(jnp.sum(lp[0:1] * lp[1:2], axis=-1, keepdims=True))
           - jnp.exp(jnp.sum(lp[2:3] * lp[3:4], axis=-1, keepdims=True)) + lambda_init)
    out = o[:, :tq] - lam * o[:, tq:]
    ms = jnp.mean(out * out, axis=0, keepdims=True)
    y = out * lax.rsqrt(ms + EPS) * g_ref[...] * (1.0 - lambda_init)
    o_ref[...] = y.T.astype(o_ref.dtype)


def _diff_bias(rel_bias):
    t = TB_DIFF
    assert t >= MAX_DISTANCE
    kk = jnp.arange(t)[:, None]
    qq = jnp.arange(t)[None, :]
    rel = jnp.stack([d * t + kk - qq for d in range(-2, 3)])
    return (_bias_lookup(rel_bias, rel) * LOG2E).astype(BF16)


def _diff_attention(qt, k, vt, bias, diff_lambda, subln_g, seq_len, lambda_init):
    n = k.shape[0]
    tq, tk = TQ_DIFF, TK_DIFF
    assert tq == tk >= MAX_DISTANCE and seq_len % tk == 0 and seq_len // tk >= 3 and DIFF_UNROLL % 2 == 0
    assert tq % TB_DIFF == 0 and tk % TB_DIFF == 0
    nb = n // seq_len
    nq = seq_len // tq
    nk = seq_len // tk
    return pl.pallas_call(
        functools.partial(_diff_kernel, lambda_init=lambda_init),
        grid=(nb, DIFF_HEADS, nq),
        in_specs=[pl.BlockSpec((DIFF_VDIM, tq), lambda b, h, i: (h, b * nq + i)),
                  pl.BlockSpec((seq_len, DIFF_VDIM), lambda b, h, i: (b, h)),
                  pl.BlockSpec((nk, DIFF_VDIM, tk), lambda b, h, i: (b, h, 0)),
                  pl.BlockSpec((None, 5, TB_DIFF, TB_DIFF), lambda b, h, i: (h, 0, 0, 0)),
                  pl.BlockSpec((4, HEAD_DIM), lambda b, h, i: (0, 0)),
                  pl.BlockSpec((DIFF_VDIM, 1), lambda b, h, i: (0, 0))],
        out_specs=pl.BlockSpec((tq, DIFF_VDIM), lambda b, h, i: (b * nq + i, h)),
        out_shape=jax.ShapeDtypeStruct((n, DIFF_HEADS * DIFF_VDIM), BF16),
        scratch_shapes=[pltpu.VMEM((DIFF_VDIM, 2 * tq), BF16),
                        pltpu.VMEM((tk, 2 * tq), BF16),
                        pltpu.VMEM((tk, 2 * tq), BF16),
                        pltpu.VMEM((tk, 2 * tq), BF16),
                        pltpu.VMEM((tk, 2 * tq), BF16),
                        pltpu.VMEM((DIFF_VDIM + BF16_SUBLANES, 2 * tq), F32)],
        compiler_params=_cparams("parallel", "parallel", "arbitrary"),
        name="diff_attention",
    )(qt, k, vt, bias, diff_lambda.astype(F32), subln_g.reshape(DIFF_VDIM, 1).astype(F32))


def _router_kernel(x_ref, gi_ref, rw_ref, rb_ref, ti_ref, tw_ref):
    h = _rms(x_ref[...], gi_ref[...]).astype(BF16)
    logits = jnp.dot(h, rw_ref[...], preferred_element_type=F32) + rb_ref[...]
    col = lax.broadcasted_iota(jnp.int32, logits.shape, 1)
    m1 = jnp.max(logits, axis=-1, keepdims=True)
    i1 = jnp.min(jnp.where(logits == m1, col, LANES), axis=-1, keepdims=True)
    rest = jnp.where(col == i1, 2 * NEG, logits)
    m2 = jnp.max(rest, axis=-1, keepdims=True)
    i2 = jnp.min(jnp.where(rest == m2, col, LANES), axis=-1, keepdims=True)
    e2 = jnp.exp(m2 - m1)
    den = 1.0 + e2
    ti_ref[...] = jnp.concatenate([i1, i2], axis=1)
    tw_ref[...] = jnp.concatenate([1.0 / den, e2 / den], axis=1)


def _router(x, g_in, router_w, router_b):
    n, d = x.shape
    ne = router_w.shape[1]
    rw = jnp.zeros((d, LANES), BF16).at[:, :ne].set(router_w.astype(BF16))
    rb = jnp.full((1, LANES), NEG, F32).at[0, :ne].set(router_b.astype(F32))
    return pl.pallas_call(
        _router_kernel,
        grid=(n // TM,),
        in_specs=[pl.BlockSpec((TM, d), lambda i: (i, 0)),
                  pl.BlockSpec((1, d), lambda i: (0, 0)),
                  pl.BlockSpec((d, LANES), lambda i: (0, 0)),
                  pl.BlockSpec((1, LANES), lambda i: (0, 0))],
        out_specs=[pl.BlockSpec((TM, 2), lambda i: (i, 0)), pl.BlockSpec((TM, 2), lambda i: (i, 0))],
        out_shape=[jax.ShapeDtypeStruct((n, 2), jnp.int32), jax.ShapeDtypeStruct((n, 2), F32)],
        compiler_params=_cparams("parallel"),
        name="moe_router",
    )(x, g_in.reshape(1, d), rw, rb)


def _route(topi, ne, t):
    n = topi.shape[0]
    p = 2 * n
    e = topi.reshape(p)
    experts = jnp.arange(ne, dtype=jnp.int32)
    onehot = (e[:, None] == experts[None, :]).astype(jnp.int32)
    csum = jnp.cumsum(onehot, axis=0)
    rank = jnp.sum(onehot * csum, axis=1) - 1
    cnt = csum[-1]
    padded = ((cnt + t - 1) // t) * t
    ends = jnp.cumsum(padded)
    starts = ends - padded
    dest = jnp.sum(onehot * starts[None, :], axis=1) + rank
    n_tiles = p // t + ne
    tile_expert = jnp.sum(jnp.arange(n_tiles, dtype=jnp.int32)[:, None] * t >= ends[None, :], axis=1)
    tile_expert = jnp.minimum(tile_expert, ne - 1).astype(jnp.int32)
    _, tok_sorted = lax.sort_key_val(dest.astype(jnp.int32), jnp.arange(p, dtype=jnp.int32) // 2)
    pad_before = starts - (jnp.cumsum(cnt) - cnt)
    row_expert = jnp.repeat(tile_expert, t)
    row_shift = jnp.sum((row_expert[:, None] == experts[None, :]) * pad_before[None, :], axis=1)
    src = jnp.take(tok_sorted, jnp.clip(jnp.arange(n_tiles * t, dtype=jnp.int32) - row_shift, 0, p - 1))
    dest_tiles = dest.reshape(n // TM, TM, 2).transpose(0, 2, 1).reshape(n // TM, 2 * TM)
    return src.reshape(n_tiles, t), tile_expert, dest_tiles.astype(jnp.int32)


def _row_copy(src_hbm, row, dst_ref, r, sem):
    return pltpu.make_async_copy(src_hbm.at[pl.ds(row, 1)], dst_ref.at[pl.ds(r, 1)], sem)


def _start_row_gather(idx_ref, islot, base, n_rows, src_hbm, dst_ref, sem):
    def body(r, c):
        _row_copy(src_hbm, idx_ref[islot, base + r], dst_ref, r, sem).start()
        return c
    lax.fori_loop(0, n_rows, body, 0, unroll=8)


def _wait_row_gather(n_rows, src_hbm, dst_ref, sem):
    def body(r, c):
        _row_copy(src_hbm, 0, dst_ref, 0, sem).wait()
        return c
    lax.fori_loop(0, n_rows, body, 0, unroll=8)


def _prefetch_schedule(t, n_steps, idx_hbm, idx_ref, isem, start_gather):
    def idx_copy(step):
        return pltpu.make_async_copy(idx_hbm.at[step], idx_ref.at[step % 3], isem.at[step % 3])

    @pl.when(t == 0)
    def _():
        idx_copy(0).start()
        idx_copy(0).wait()
        start_gather(0)
        if n_steps > 1:
            idx_copy(1).start()

    @pl.when(t + 1 < n_steps)
    def _():
        idx_copy(t + 1).wait()

        @pl.when(t + 2 < n_steps)
        def _():
            idx_copy(t + 2).start()

        start_gather(t + 1)


def _moe_expert_kernel(te_ref, src_hbm, x_hbm, gi_ref, wg_ref, wu_ref, wd_ref, o_ref,
                       idx_ref, xg_ref, isem, gsem, h_ref, acc_ref):
    t = pl.program_id(0)
    j = pl.program_id(1)
    rows = xg_ref.shape[1]

    @pl.when(j == 0)
    def _():
        def start_gather(step):
            _start_row_gather(idx_ref, step % 3, 0, rows, x_hbm, xg_ref.at[step % 2], gsem.at[step % 2])

        _prefetch_schedule(t, src_hbm.shape[0], src_hbm, idx_ref, isem, start_gather)
        _wait_row_gather(rows, x_hbm, xg_ref.at[t % 2], gsem.at[t % 2])
        h_ref[...] = _rms(xg_ref[t % 2], gi_ref[...]).astype(BF16)
        acc_ref[...] = jnp.zeros_like(acc_ref)

    h = h_ref[...]
    a = jnp.dot(h, wg_ref[...], preferred_element_type=F32)
    b = jnp.dot(h, wu_ref[...], preferred_element_type=F32)
    mid = (a * jax.nn.sigmoid(a) * b).astype(BF16)
    acc_ref[...] += jnp.dot(mid, wd_ref[...], preferred_element_type=F32)

    @pl.when(j == pl.num_programs(1) - 1)
    def _():
        o_ref[...] = acc_ref[...]


def _moe_experts(x, g_in, src, tile_expert, wg, wu, wd):
    n, d = x.shape
    n_tiles, t = src.shape
    nf = wg.shape[2] // TF
    return pl.pallas_call(
        _moe_expert_kernel,
        grid_spec=pltpu.PrefetchScalarGridSpec(
            num_scalar_prefetch=1,
            grid=(n_tiles, nf),
            in_specs=[pl.BlockSpec(memory_space=pl.ANY),
                      pl.BlockSpec(memory_space=pl.ANY),
                      pl.BlockSpec((1, d), lambda i, j, te: (0, 0)),
                      pl.BlockSpec((None, d, TF), lambda i, j, te: (te[i], 0, j)),
                      pl.BlockSpec((None, d, TF), lambda i, j, te: (te[i], 0, j)),
                      pl.BlockSpec((None, TF, d), lambda i, j, te: (te[i], j, 0))],
            out_specs=pl.BlockSpec((t, d), lambda i, j, te: (i, 0)),
            scratch_shapes=[pltpu.SMEM((3, t), jnp.int32),
                            pltpu.VMEM((2, t, d), F32),
                            pltpu.SemaphoreType.DMA((3,)),
                            pltpu.SemaphoreType.DMA((2,)),
                            pltpu.VMEM((t, d), BF16),
                            pltpu.VMEM((t, d), F32)]),
        out_shape=jax.ShapeDtypeStruct((n_tiles * t, d), F32),
        compiler_params=_cparams("arbitrary", "arbitrary"),
        name="moe_experts",
    )(tile_expert, src, x, g_in.reshape(1, d), wg, wu, wd)


def _moe_combine_kernel(dest_hbm, y_hbm, tw_ref, x_ref, go_ref, o_ref, idx_ref, buf_ref, isem, gsem):
    i = pl.program_id(0)
    tm = x_ref.shape[0]

    def start_gather(step):
        for slot in range(2):
            _start_row_gather(idx_ref, step % 3, slot * tm, tm, y_hbm,
                              buf_ref.at[step % 2, slot], gsem.at[step % 2])

    _prefetch_schedule(i, dest_hbm.shape[0], dest_hbm, idx_ref, isem, start_gather)
    for slot in range(2):
        _wait_row_gather(tm, y_hbm, buf_ref.at[i % 2, slot], gsem.at[i % 2])
    tw = tw_ref[...]
    y = tw[:, 0:1] * buf_ref[i % 2, 0] + tw[:, 1:2] * buf_ref[i % 2, 1]
    o_ref[...] = x_ref[...] + _rms(y, go_ref[...])


def _moe_combine(y, dest_tiles, topw, x, g_out):
    n, d = x.shape
    return pl.pallas_call(
        _moe_combine_kernel,
        grid=(n // TM,),
        in_specs=[pl.BlockSpec(memory_space=pl.ANY),
                  pl.BlockSpec(memory_space=pl.ANY),
                  pl.BlockSpec((TM, 2), lambda i: (i, 0)),
                  pl.BlockSpec((TM, d), lambda i: (i, 0)),
                  pl.BlockSpec((1, d), lambda i: (0, 0))],
        out_specs=pl.BlockSpec((TM, d), lambda i: (i, 0)),
        out_shape=jax.ShapeDtypeStruct((n, d), F32),
        scratch_shapes=[pltpu.SMEM((3, 2 * TM), jnp.int32),
                        pltpu.VMEM((2, 2, TM, d), F32),
                        pltpu.SemaphoreType.DMA((3,)),
                        pltpu.SemaphoreType.DMA((2,))],
        compiler_params=_cparams("arbitrary"),
        name="moe_combine",
    )(dest_tiles, y, topw, x, g_out.reshape(1, d))


def _moe(x, g_in, router_w, router_b, wg, wu, wd, g_out):
    topi, topw = _router(x, g_in, router_w, router_b)
    src, tile_expert, dest_tiles = _route(topi, wg.shape[0], TM)
    y = _moe_experts(x, g_in, src, tile_expert, wg, wu, wd)
    return _moe_combine(y, dest_tiles, topw, x, g_out)


def _trunk(x, seq_len, p):
    n = x.shape[0]
    depth = p["norm_mix"].shape[0]
    for layer in range(depth):
        j = layer // 2
        if layer % 2 == 0:
            o_q = POOL_WIDTH
            o_kv = o_q + WIN_Q_HEADS * HEAD_DIM
            o_end = o_kv + 2 * WIN_KV_HEADS * HEAD_DIM
            u, q, kv = _norm_matmul(x, p["norm_mix"][layer, 0], p["w_in0"][j],
                                    ((0, o_q), (o_q, o_kv), (o_kv, o_end)), (F32, BF16, BF16))
            a = _pool_mixer(u, p["pool_w"][j], p["pool_scale"][j], seq_len)
            b = _window_attention(q, kv, p["win_bias"], p["sink"][j], seq_len)
            w_out = p["w_out0"][j]
            x = _proj_norm_res([a, b], [w_out[:POOL_WIDTH], w_out[POOL_WIDTH:]], x, p["norm_mix"][layer, 1])
            x = _ffn(x, p["norm_ffn"][layer, 0], p["ffn_wg"][j], p["ffn_wu"][j], p["ffn_wd"][j],
                     p["norm_ffn"][layer, 1])
        else:
            lambda_init = 0.8 - 0.6 * math.exp(-0.3 * layer)
            dq = DIFF_HEADS * 2 * HEAD_DIM
            q, k, v = _norm_matmul(x, p["norm_mix"][layer, 0], p["w_in1"][j],
                                   ((0, dq), (dq, 2 * dq), (2 * dq, 3 * dq)), (BF16, BF16, BF16))
            qt = q.T
            vt = v.reshape(n // TK_DIFF, TK_DIFF, dq).transpose(0, 2, 1)
            m = _diff_attention(qt, k, vt, p["diff_bias"], p["diff_lambda"][j], p["subln_g"][j],
                                seq_len, lambda_init)
            x = _proj_norm_res([m], [p["w_out1"][j]], x, p["norm_mix"][layer, 1])
            x = _moe(x, p["norm_ffn"][layer, 0], p["router_w"][j], p["router_b"][j],
                     p["moe_wg"][j], p["moe_wu"][j], p["moe_wd"][j], p["norm_ffn"][layer, 1])
    return x


def _prepare(norm_mix, norm_ffn, rel_bias, w_in0, pool_w, pool_scale, sink, w_out0,
             ffn_wg, ffn_wu, ffn_wd, w_in1, diff_lambda, subln_g, w_out1,
             router_w, router_b, moe_wg, moe_wu, moe_wd):
    scale = HEAD_DIM ** -0.5
    o_q = POOL_WIDTH
    o_kv = o_q + WIN_Q_HEADS * HEAD_DIM
    col0 = jnp.arange(w_in0.shape[-1])
    s0 = jnp.where((col0 >= o_q) & (col0 < o_kv), scale, 1.0).astype(F32)
    col1 = jnp.arange(w_in1.shape[-1])
    s1 = jnp.where(col1 < DIFF_HEADS * 2 * HEAD_DIM, scale * LOG2E, 1.0).astype(F32)
    return dict(
        norm_mix=norm_mix.astype(F32), norm_ffn=norm_ffn.astype(F32),
        win_bias=_window_bias(rel_bias, TQ_WIN), diff_bias=_diff_bias(rel_bias),
        w_in0=(w_in0 * s0).astype(BF16), pool_w=pool_w.astype(BF16), pool_scale=pool_scale.astype(F32),
        sink=sink, w_out0=w_out0.astype(BF16),
        ffn_wg=ffn_wg.astype(BF16), ffn_wu=ffn_wu.astype(BF16), ffn_wd=ffn_wd.astype(BF16),
        w_in1=(w_in1 * s1).astype(BF16), diff_lambda=diff_lambda, subln_g=subln_g,
        w_out1=w_out1.astype(BF16), router_w=router_w, router_b=router_b,
        moe_wg=moe_wg.astype(BF16), moe_wu=moe_wu.astype(BF16), moe_wd=moe_wd.astype(BF16))


def kernel(x_prompt, x_sample, norm_mix, norm_ffn, rel_bias, w_in0, pool_w, pool_scale, sink, w_out0,
           ffn_wg, ffn_wu, ffn_wd, w_in1, diff_lambda, subln_g, w_out1,
           router_w, router_b, moe_wg, moe_wu, moe_wd):
    p = _prepare(norm_mix, norm_ffn, rel_bias, w_in0, pool_w, pool_scale, sink, w_out0,
                 ffn_wg, ffn_wu, ffn_wd, w_in1, diff_lambda, subln_g, w_out1,
                 router_w, router_b, moe_wg, moe_wu, moe_wd)
    outs = []
    for x in (x_prompt, x_sample):
        b, s, d = x.shape
        outs.append(_trunk(x.reshape(b * s, d), s, p).reshape(b, s, d))
    return tuple(outs)
```

```python
import functools
import math

import jax
import jax.numpy as jnp
from jax import lax
from jax.experimental import pallas as pl
from jax.experimental.pallas import tpu as pltpu

D_MODEL = 1024
HEAD_DIM = 64
POOL_WIDTH = 512
POOL_WINDOWS = (2, 4, 8, 16)
POOL_GW = 128
POOL_HALO = 8
WIN_Q_HEADS = 8
WIN_KV_HEADS = 2
WIN_GROUP = 4
WINDOW = 128
DIFF_HEADS = 8
DIFF_VDIM = 128
NUM_BUCKETS = 32
MAX_DISTANCE = 128
D_FF = 2816
N_EXPERTS = 8
EPS = 1e-6
NEG = -1e30
LOG2E = 1.4426950408889634

LANES = 128
BF16_SUBLANES = 16
VMEM_LIMIT = 56 * 1024 * 1024

TM = 512
TF = 1408
TQ_WIN = 512
TQ_DIFF = 1024
TK_DIFF = 1024
TB_DIFF = 256
DIFF_UNROLL = 2

F32 = jnp.float32
BF16 = jnp.bfloat16


def _cparams(*sem):
    return pltpu.CompilerParams(dimension_semantics=sem, vmem_limit_bytes=VMEM_LIMIT)


def _rms(xf, g):
    ms = jnp.mean(xf * xf, axis=-1, keepdims=True)
    return xf * lax.rsqrt(ms + EPS) * g


def _t5_bucket(rel):
    half = NUM_BUCKETS // 2
    max_exact = half // 2
    ret = jnp.where(rel > 0, half, 0)
    n = jnp.abs(rel)
    nf = jnp.maximum(n, 1).astype(jnp.float32)
    large = max_exact + (jnp.log(nf / max_exact) / math.log(MAX_DISTANCE / max_exact)
                         * (half - max_exact)).astype(jnp.int32)
    large = jnp.minimum(large, half - 1)
    return ret + jnp.where(n < max_exact, n, large)


def _bias_lookup(rel_bias, rel):
    onehot = (_t5_bucket(rel)[..., None] == jnp.arange(NUM_BUCKETS)).astype(F32)
    out = jnp.einsum("...b,bh->h...", onehot, rel_bias.astype(F32), precision=lax.Precision.HIGHEST)
    return out


def _norm_matmul_kernel(x_ref, g_ref, w_ref, *o_refs, splits):
    h = _rms(x_ref[...], g_ref[...]).astype(BF16)
    z = jnp.dot(h, w_ref[...], preferred_element_type=F32)
    for o_ref, (a, b) in zip(o_refs, splits):
        o_ref[...] = z[:, a:b].astype(o_ref.dtype)


def _norm_matmul(x, g, w, splits, dtypes):
    n, d = x.shape
    nout = w.shape[1]
    return pl.pallas_call(
        functools.partial(_norm_matmul_kernel, splits=splits),
        grid=(n // TM,),
        in_specs=[pl.BlockSpec((TM, d), lambda i: (i, 0)),
                  pl.BlockSpec((1, d), lambda i: (0, 0)),
                  pl.BlockSpec((d, nout), lambda i: (0, 0))],
        out_specs=[pl.BlockSpec((TM, b - a), lambda i: (i, 0)) for a, b in splits],
        out_shape=[jax.ShapeDtypeStruct((n, b - a), dt) for (a, b), dt in zip(splits, dtypes)],
        compiler_params=_cparams("parallel"),
        name="norm_matmul",
    )(x, g.reshape(1, d), w)


def _proj_norm_res_kernel(*refs, n_in):
    ins, ws = refs[:n_in], refs[n_in:2 * n_in]
    x_ref, g_ref, o_ref = refs[2 * n_in:]
    m = jnp.dot(ins[0][...], ws[0][...], preferred_element_type=F32)
    for a_ref, w_ref in zip(ins[1:], ws[1:]):
        m = m + jnp.dot(a_ref[...], w_ref[...], preferred_element_type=F32)
    o_ref[...] = x_ref[...] + _rms(m, g_ref[...])


def _proj_norm_res(ins, ws, x, g):
    n, d = x.shape
    n_in = len(ins)
    return pl.pallas_call(
        functools.partial(_proj_norm_res_kernel, n_in=n_in),
        grid=(n // TM,),
        in_specs=([pl.BlockSpec((TM, a.shape[1]), lambda i: (i, 0)) for a in ins]
                  + [pl.BlockSpec(w.shape, lambda i: (0, 0)) for w in ws]
                  + [pl.BlockSpec((TM, d), lambda i: (i, 0)),
                     pl.BlockSpec((1, d), lambda i: (0, 0))]),
        out_specs=pl.BlockSpec((TM, d), lambda i: (i, 0)),
        out_shape=jax.ShapeDtypeStruct((n, d), F32),
        compiler_params=_cparams("parallel"),
        name="proj_norm_res",
    )(*ins, *ws, x, g.reshape(1, d))


def _pool_kernel(up_ref, u_ref, un_ref, w_ref, sc_ref, o_ref, ext_ref, *, seq_len):
    tm = u_ref.shape[0]
    t0 = (pl.program_id(0) * tm) % seq_len
    ext_ref[0:POOL_HALO, :] = jnp.where(t0 > 0, up_ref[...], 0.0)
    ext_ref[POOL_HALO:POOL_HALO + tm, :] = u_ref[...]
    ext_ref[POOL_HALO + tm:, :] = jnp.where(t0 + tm < seq_len, un_ref[...], 0.0)
    t = t0 + lax.broadcasted_iota(jnp.int32, (tm, 1), 0)
    outs = []
    for g, w in enumerate(POOL_WINDOWS):
        half = w // 2
        sl = slice(g * POOL_GW, (g + 1) * POOL_GW)
        acc = ext_ref[POOL_HALO - half:POOL_HALO - half + tm, sl]
        for j in range(-half + 1, half):
            acc = acc + ext_ref[POOL_HALO + j:POOL_HALO + j + tm, sl]
        lo = jnp.maximum(t - half, 0)
        hi = jnp.minimum(t + half - 1, seq_len - 1)
        cnt = (hi - lo + 1).astype(F32)
        d = (acc / cnt - u_ref[:, sl]).astype(BF16)
        outs.append(jnp.dot(d, w_ref[g], preferred_element_type=F32))
    o_ref[...] = (jnp.concatenate(outs, axis=-1) * sc_ref[...]).astype(o_ref.dtype)


def _pool_mixer(u, pool_w, pool_scale, seq_len):
    n, c = u.shape
    hb = TM // POOL_HALO
    last = n // POOL_HALO - 1
    return pl.pallas_call(
        functools.partial(_pool_kernel, seq_len=seq_len),
        grid=(n // TM,),
        in_specs=[pl.BlockSpec((POOL_HALO, c), lambda i: (jnp.maximum(i * hb - 1, 0), 0)),
                  pl.BlockSpec((TM, c), lambda i: (i, 0)),
                  pl.BlockSpec((POOL_HALO, c), lambda i: (jnp.minimum((i + 1) * hb, last), 0)),
                  pl.BlockSpec(pool_w.shape, lambda i: (0, 0, 0)),
                  pl.BlockSpec((1, c), lambda i: (0, 0))],
        out_specs=pl.BlockSpec((TM, c), lambda i: (i, 0)),
        out_shape=jax.ShapeDtypeStruct((n, c), BF16),
        scratch_shapes=[pltpu.VMEM((TM + 2 * POOL_HALO, c), F32)],
        compiler_params=_cparams("parallel"),
        name="pool_mixer",
    )(u, u, u, pool_w, pool_scale.reshape(1, c))


def _win_kernel(q_ref, kvp_ref, kv_ref, kvn_ref, bias_ref, sink_ref, o_ref, *, seq_len):
    tq = q_ref.shape[0]
    nk = tq + 2 * WINDOW
    t0 = (pl.program_id(0) * tq) % seq_len
    kv = jnp.concatenate([kvp_ref[...], kv_ref[...], kvn_ref[...]], axis=0)
    kpos = t0 - WINDOW + lax.broadcasted_iota(jnp.int32, (1, nk), 1)
    valid = (kpos >= 0) & (kpos < seq_len)
    q = q_ref[...]
    sink = sink_ref[...]
    heads = range(WIN_Q_HEADS)
    scores = []
    for h in heads:
        g = h // WIN_GROUP
        qh = q[:, h * HEAD_DIM:(h + 1) * HEAD_DIM]
        k = kv[:, g * HEAD_DIM:(g + 1) * HEAD_DIM]
        s = lax.dot_general(qh, k, (((1,), (1,)), ((), ())), preferred_element_type=F32)
        scores.append(jnp.where(valid, s + bias_ref[h], NEG))
    probs = []
    for h in heads:
        s = scores[h]
        sk = sink[:, h:h + 1]
        m = jnp.maximum(jnp.max(s, axis=-1, keepdims=True), sk)
        e = jnp.exp(s - m)
        l = jnp.sum(e, axis=-1, keepdims=True) + jnp.exp(sk - m)
        probs.append((e / l).astype(BF16))
    outs = []
    for h in heads:
        g = h // WIN_GROUP
        v = kv[:, (WIN_KV_HEADS + g) * HEAD_DIM:(WIN_KV_HEADS + g + 1) * HEAD_DIM]
        outs.append(jnp.dot(probs[h], v, preferred_element_type=F32))
    o_ref[...] = jnp.concatenate(outs, axis=-1).astype(o_ref.dtype)


def _window_bias(rel_bias, tq):
    nk = tq + 2 * WINDOW
    rel = jnp.arange(nk)[None, :] - WINDOW - jnp.arange(tq)[:, None]
    return jnp.where((jnp.abs(rel) <= WINDOW)[None], _bias_lookup(rel_bias, rel), NEG)


def _window_attention(q, kv, bias, sink, seq_len):
    n = q.shape[0]
    tq = TQ_WIN
    hb = tq // WINDOW
    last = n // WINDOW - 1
    return pl.pallas_call(
        functools.partial(_win_kernel, seq_len=seq_len),
        grid=(n // tq,),
        in_specs=[pl.BlockSpec((tq, q.shape[1]), lambda i: (i, 0)),
                  pl.BlockSpec((WINDOW, kv.shape[1]), lambda i: (jnp.maximum(i * hb - 1, 0), 0)),
                  pl.BlockSpec((tq, kv.shape[1]), lambda i: (i, 0)),
                  pl.BlockSpec((WINDOW, kv.shape[1]), lambda i: (jnp.minimum((i + 1) * hb, last), 0)),
                  pl.BlockSpec(bias.shape, lambda i: (0, 0, 0)),
                  pl.BlockSpec((1, WIN_Q_HEADS), lambda i: (0, 0))],
        out_specs=pl.BlockSpec((tq, q.shape[1]), lambda i: (i, 0)),
        out_shape=jax.ShapeDtypeStruct(q.shape, BF16),
        compiler_params=_cparams("parallel"),
        name="window_attention",
    )(q, kv, kv, kv, bias, sink.reshape(1, WIN_Q_HEADS).astype(F32))


def _ffn_kernel(x_ref, gi_ref, wg_ref, wu_ref, wd_ref, go_ref, o_ref, h_ref, acc_ref):
    j = pl.program_id(1)

    @pl.when(j == 0)
    def _():
        h_ref[...] = _rms(x_ref[...], gi_ref[...]).astype(BF16)
        acc_ref[...] = jnp.zeros_like(acc_ref)

    h = h_ref[...]
    a = jnp.dot(h, wg_ref[...], preferred_element_type=F32)
    b = jnp.dot(h, wu_ref[...], preferred_element_type=F32)
    mid = (a * jax.nn.sigmoid(a) * b).astype(BF16)
    acc_ref[...] += jnp.dot(mid, wd_ref[...], preferred_element_type=F32)

    @pl.when(j == pl.num_programs(1) - 1)
    def _():
        o_ref[...] = x_ref[...] + _rms(acc_ref[...], go_ref[...])


def _ffn(x, g_in, wg, wu, wd, g_out):
    n, d = x.shape
    nf = wg.shape[1] // TF
    return pl.pallas_call(
        _ffn_kernel,
        grid=(n // TM, nf),
        in_specs=[pl.BlockSpec((TM, d), lambda i, j: (i, 0)),
                  pl.BlockSpec((1, d), lambda i, j: (0, 0)),
                  pl.BlockSpec((d, TF), lambda i, j: (0, j)),
                  pl.BlockSpec((d, TF), lambda i, j: (0, j)),
                  pl.BlockSpec((TF, d), lambda i, j: (j, 0)),
                  pl.BlockSpec((1, d), lambda i, j: (0, 0))],
        out_specs=pl.BlockSpec((TM, d), lambda i, j: (i, 0)),
        out_shape=jax.ShapeDtypeStruct((n, d), F32),
        scratch_shapes=[pltpu.VMEM((TM, d), BF16), pltpu.VMEM((TM, d), F32)],
        compiler_params=_cparams("parallel", "arbitrary"),
        name="ffn",
    )(x, g_in.reshape(1, d), wg, wu, wd, g_out.reshape(1, d))


def _diff_kernel(qt_ref, k_ref, vt_ref, bias_ref, lam_ref, g_ref, o_ref,
                 qbd_ref, s0_ref, s1_ref, e0_ref, e1_ref, acc_ref, *, lambda_init):
    tq = qt_ref.shape[1]
    nk, _, tk = vt_ref.shape
    i = pl.program_id(2)
    qt = qt_ref[...]
    zero = jnp.zeros((HEAD_DIM, tq), BF16)
    qbd_ref[0:HEAD_DIM, 0:tq] = qt[0:HEAD_DIM]
    qbd_ref[0:HEAD_DIM, tq:] = zero
    qbd_ref[HEAD_DIM:, 0:tq] = zero
    qbd_ref[HEAD_DIM:, tq:] = qt[HEAD_DIM:]
    acc_ref[...] = jnp.zeros_like(acc_ref)
    ones = jnp.ones((BF16_SUBLANES, tk), BF16)

    n_near = 3
    base = jnp.clip(i - 1, 0, nk - n_near)
    bias_left = bias_ref[0, 0:1, 0:1].astype(F32)
    bias_right = bias_ref[4, 0:1, 0:1].astype(F32)

    def chunk_at(p):
        far = p - n_near
        return jnp.where(p < n_near, base + p, jnp.where(far < base, far, far + n_near))

    def scores_far(j, dst_ref):
        kc = k_ref[pl.ds(pl.multiple_of(j * tk, tk), tk), :]
        s = jnp.dot(kc, qbd_ref[...], preferred_element_type=F32).astype(BF16)
        dst_ref[...] = s
        off = jnp.where(j < i, bias_left, bias_right)
        return jnp.max(s, axis=0, keepdims=True).astype(F32) + off, off

    def scores(j, dst_ref):
        kc = k_ref[pl.ds(pl.multiple_of(j * tk, tk), tk), :]
        s = jnp.dot(kc, qbd_ref[...], preferred_element_type=F32).astype(BF16)
        rows = []
        for c in range(tk // TB_DIFF):
            sc = s[c * TB_DIFF:(c + 1) * TB_DIFF]
            cols = []
            for mp in range(2):
                for a in range(tq // TB_DIFF):
                    d = jnp.clip(j * (tk // TB_DIFF) + c - (i * (tq // TB_DIFF) + a), -2, 2) + 2
                    c0 = mp * tq + a * TB_DIFF
                    cols.append(sc[:, c0:c0 + TB_DIFF] + bias_ref[d])
            rows.append(jnp.concatenate(cols, axis=1))
        s = jnp.concatenate(rows, axis=0)
        dst_ref[...] = s
        return jnp.max(s, axis=0, keepdims=True).astype(F32), jnp.zeros((1, 1), F32)

    def softmax(src_ref, dst_ref, cmax, off, m_old):
        m_new = jnp.maximum(m_old, cmax)
        alpha = jnp.exp2(m_old - m_new)
        dst_ref[...] = jnp.exp2(src_ref[...] - (m_new - off).astype(BF16))
        return m_new, alpha

    def accumulate(j, src_ref, alpha):
        vt = jnp.concatenate([vt_ref[j], ones], axis=0)
        acc_ref[...] = alpha * acc_ref[...] + jnp.dot(vt, src_ref[...], preferred_element_type=F32)

    s_refs = (s0_ref, s1_ref)
    e_refs = (e0_ref, e1_ref)

    def step(p, parity, next_scores, carry):
        m, a_prev, c_cur, off_cur = carry
        c_next, off_next = next_scores(chunk_at(p + 1), s_refs[1 - parity])
        m, a_cur = softmax(s_refs[parity], e_refs[parity], c_cur, off_cur, m)
        accumulate(chunk_at(p - 1), e_refs[1 - parity], a_prev)
        return m, a_cur, c_next, off_next

    m = jnp.full((1, 2 * tq), NEG, F32)
    c0, off0 = scores(base, s0_ref)
    c1, off1 = scores(base + 1, s1_ref)
    m, a0 = softmax(s0_ref, e0_ref, c0, off0, m)
    carry = step(1, 1, scores, (m, a0, c1, off1))

    n_trips = (nk - n_near) // DIFF_UNROLL

    def trip(jj, carry):
        for u in range(DIFF_UNROLL):
            carry = step(DIFF_UNROLL * jj + 2 + u, u % 2, scores_far, carry)
        return carry

    carry = lax.fori_loop(0, n_trips, trip, carry)
    for p in range(2 + DIFF_UNROLL * n_trips, nk - 1):
        carry = step(p, p % 2, scores_far, carry)
    m, a_prev, c_last, off_last = carry
    last = (nk - 1) % 2
    m, a_last = softmax(s_refs[last], e_refs[last], c_last, off_last, m)
    accumulate(chunk_at(nk - 2), e_refs[1 - last], a_prev)
    accumulate(chunk_at(nk - 1), e_refs[last], a_last)

    acc = acc_ref[...]
    o = acc[:DIFF_VDIM] / acc[DIFF_VDIM:DIFF_VDIM + 1]
    lp = lam_ref[...]
    lam = (jnp.exp(jnp.sum(lp[0:1] * lp[1:2], axis=-1, keepdims=True))
           - jnp.exp(jnp.sum(lp[2:3] * lp[3:4], axis=-1, keepdims=True)) + lambda_init)
    out = o[:, :tq] - lam * o[:, tq:]
    ms = jnp.mean(out * out, axis=0, keepdims=True)
    y = out * lax.rsqrt(ms + EPS) * g_ref[...] * (1.0 - lambda_init)
    o_ref[...] = y.T.astype(o_ref.dtype)


def _diff_bias(rel_bias):
    t = TB_DIFF
    assert t >= MAX_DISTANCE
    kk = jnp.arange(t)[:, None]
    qq = jnp.arange(t)[None, :]
    rel = jnp.stack([d * t + kk - qq for d in range(-2, 3)])
    return (_bias_lookup(rel_bias, rel) * LOG2E).astype(BF16)


def _diff_attention(qt, k, vt, bias, diff_lambda, subln_g, seq_len, lambda_init):
    n = k.shape[0]
    tq, tk = TQ_DIFF, TK_DIFF
    assert tq == tk >= MAX_DISTANCE and seq_len % tk == 0 and seq_len // tk >= 3 and DIFF_UNROLL % 2 == 0
    assert tq % TB_DIFF == 0 and tk % TB_DIFF == 0
    nb = n // seq_len
    nq = seq_len // tq
    nk = seq_len // tk
    return pl.pallas_call(
        functools.partial(_diff_kernel, lambda_init=lambda_init),
        grid=(nb, DIFF_HEADS, nq),
        in_specs=[pl.BlockSpec((DIFF_VDIM, tq), lambda b, h, i: (h, b * nq + i)),
                  pl.BlockSpec((seq_len, DIFF_VDIM), lambda b, h, i: (b, h)),
                  pl.BlockSpec((nk, DIFF_VDIM, tk), lambda b, h, i: (b, h, 0)),
                  pl.BlockSpec((None, 5, TB_DIFF, TB_DIFF), lambda b, h, i: (h, 0, 0, 0)),
                  pl.BlockSpec((4, HEAD_DIM), lambda b, h, i: (0, 0)),
                  pl.BlockSpec((DIFF_VDIM, 1), lambda b, h, i: (0, 0))],
        out_specs=pl.BlockSpec((tq, DIFF_VDIM), lambda b, h, i: (b * nq + i, h)),
        out_shape=jax.ShapeDtypeStruct((n, DIFF_HEADS * DIFF_VDIM), BF16),
        scratch_shapes=[pltpu.VMEM((DIFF_VDIM, 2 * tq), BF16),
                        pltpu.VMEM((tk, 2 * tq), BF16),
                        pltpu.VMEM((tk, 2 * tq), BF16),
                        pltpu.VMEM((tk, 2 * tq), BF16),
                        pltpu.VMEM((tk, 2 * tq), BF16),
                        pltpu.VMEM((DIFF_VDIM + BF16_SUBLANES, 2 * tq), F32)],
        compiler_params=_cparams("parallel", "parallel", "arbitrary"),
        name="diff_attention",
    )(qt, k, vt, bias, diff_lambda.astype(F32), subln_g.reshape(DIFF_VDIM, 1).astype(F32))


def _router_kernel(x_ref, gi_ref, rw_ref, rb_ref, ti_ref, tw_ref):
    h = _rms(x_ref[...], gi_ref[...]).astype(BF16)
    logits = jnp.dot(h, rw_ref[...], preferred_element_type=F32) + rb_ref[...]
    col = lax.broadcasted_iota(jnp.int32, logits.shape, 1)
    m1 = jnp.max(logits, axis=-1, keepdims=True)
    i1 = jnp.min(jnp.where(logits == m1, col, LANES), axis=-1, keepdims=True)
    rest = jnp.where(col == i1, 2 * NEG, logits)
    m2 = jnp.max(rest, axis=-1, keepdims=True)
    i2 = jnp.min(jnp.where(rest == m2, col, LANES), axis=-1, keepdims=True)
    e2 = jnp.exp(m2 - m1)
    den = 1.0 + e2
    ti_ref[...] = jnp.concatenate([i1, i2], axis=1)
    tw_ref[...] = jnp.concatenate([1.0 / den, e2 / den], axis=1)


def _router(x, g_in, router_w, router_b):
    n, d = x.shape
    ne = router_w.shape[1]
    rw = jnp.zeros((d, LANES), BF16).at[:, :ne].set(router_w.astype(BF16))
    rb = jnp.full((1, LANES), NEG, F32).at[0, :ne].set(router_b.astype(F32))
    return pl.pallas_call(
        _router_kernel,
        grid=(n // TM,),
        in_specs=[pl.BlockSpec((TM, d), lambda i: (i, 0)),
                  pl.BlockSpec((1, d), lambda i: (0, 0)),
                  pl.BlockSpec((d, LANES), lambda i: (0, 0)),
                  pl.BlockSpec((1, LANES), lambda i: (0, 0))],
        out_specs=[pl.BlockSpec((TM, 2), lambda i: (i, 0)), pl.BlockSpec((TM, 2), lambda i: (i, 0))],
        out_shape=[jax.ShapeDtypeStruct((n, 2), jnp.int32), jax.ShapeDtypeStruct((n, 2), F32)],
        compiler_params=_cparams("parallel"),
        name="moe_router",
    )(x, g_in.reshape(1, d), rw, rb)


def _route(topi, ne, t):
    n = topi.shape[0]
    p = 2 * n
    e = topi.reshape(p)
    experts = jnp.arange(ne, dtype=jnp.int32)
    onehot = (e[:, None] == experts[None, :]).astype(jnp.int32)
    csum = jnp.cumsum(onehot, axis=0)
    rank = jnp.sum(onehot * csum, axis=1) - 1
    cnt = csum[-1]
    padded = ((cnt + t - 1) // t) * t
    ends = jnp.cumsum(padded)
    starts = ends - padded
    dest = jnp.sum(onehot * starts[None, :], axis=1) + rank
    n_tiles = p // t + ne
    tile_expert = jnp.sum(jnp.arange(n_tiles, dtype=jnp.int32)[:, None] * t >= ends[None, :], axis=1)
    tile_expert = jnp.minimum(tile_expert, ne - 1).astype(jnp.int32)
    _, tok_sorted = lax.sort_key_val(dest.astype(jnp.int32), jnp.arange(p, dtype=jnp.int32) // 2)
    pad_before = starts - (jnp.cumsum(cnt) - cnt)
    row_expert = jnp.repeat(tile_expert, t)
    row_shift = jnp.sum((row_expert[:, None] == experts[None, :]) * pad_before[None, :], axis=1)
    src = jnp.take(tok_sorted, jnp.clip(jnp.arange(n_tiles * t, dtype=jnp.int32) - row_shift, 0, p - 1))
    dest_tiles = dest.reshape(n // TM, TM, 2).transpose(0, 2, 1).reshape(n // TM, 2 * TM)
    return src.reshape(n_tiles, t), tile_expert, dest_tiles.astype(jnp.int32)


def _row_copy(src_hbm, row, dst_ref, r, sem):
    return pltpu.make_async_copy(src_hbm.at[pl.ds(row, 1)], dst_ref.at[pl.ds(r, 1)], sem)


def _start_row_gather(idx_ref, islot, base, n_rows, src_hbm, dst_ref, sem):
    def body(r, c):
        _row_copy(src_hbm, idx_ref[islot, base + r], dst_ref, r, sem).start()
        return c
    lax.fori_loop(0, n_rows, body, 0, unroll=8)


def _wait_row_gather(n_rows, src_hbm, dst_ref, sem):
    def body(r, c):
        _row_copy(src_hbm, 0, dst_ref, 0, sem).wait()
        return c
    lax.fori_loop(0, n_rows, body, 0, unroll=8)


def _prefetch_schedule(t, n_steps, idx_hbm, idx_ref, isem, start_gather):
    def idx_copy(step):
        return pltpu.make_async_copy(idx_hbm.at[step], idx_ref.at[step % 3], isem.at[step % 3])

    @pl.when(t == 0)
    def _():
        idx_copy(0).start()
        idx_copy(0).wait()
        start_gather(0)
        if n_steps > 1:
            idx_copy(1).start()

    @pl.when(t + 1 < n_steps)
    def _():
        idx_copy(t + 1).wait()

        @pl.when(t + 2 < n_steps)
        def _():
            idx_copy(t + 2).start()

        start_gather(t + 1)


def _moe_expert_kernel(te_ref, src_hbm, x_hbm, gi_ref, wg_ref, wu_ref, wd_ref, o_ref,
                       idx_ref, xg_ref, isem, gsem, h_ref, acc_ref):
    t = pl.program_id(0)
    j = pl.program_id(1)
    rows = xg_ref.shape[1]

    @pl.when(j == 0)
    def _():
        def start_gather(step):
            _start_row_gather(idx_ref, step % 3, 0, rows, x_hbm, xg_ref.at[step % 2], gsem.at[step % 2])

        _prefetch_schedule(t, src_hbm.shape[0], src_hbm, idx_ref, isem, start_gather)
        _wait_row_gather(rows, x_hbm, xg_ref.at[t % 2], gsem.at[t % 2])
        h_ref[...] = _rms(xg_ref[t % 2], gi_ref[...]).astype(BF16)
        acc_ref[...] = jnp.zeros_like(acc_ref)

    h = h_ref[...]
    a = jnp.dot(h, wg_ref[...], preferred_element_type=F32)
    b = jnp.dot(h, wu_ref[...], preferred_element_type=F32)
    mid = (a * jax.nn.sigmoid(a) * b).astype(BF16)
    acc_ref[...] += jnp.dot(mid, wd_ref[...], preferred_element_type=F32)

    @pl.when(j == pl.num_programs(1) - 1)
    def _():
        o_ref[...] = acc_ref[...]


def _moe_experts(x, g_in, src, tile_expert, wg, wu, wd):
    n, d = x.shape
    n_tiles, t = src.shape
    nf = wg.shape[2] // TF
    return pl.pallas_call(
        _moe_expert_kernel,
        grid_spec=pltpu.PrefetchScalarGridSpec(
            num_scalar_prefetch=1,
            grid=(n_tiles, nf),
            in_specs=[pl.BlockSpec(memory_space=pl.ANY),
                      pl.BlockSpec(memory_space=pl.ANY),
                      pl.BlockSpec((1, d), lambda i, j, te: (0, 0)),
                      pl.BlockSpec((None, d, TF), lambda i, j, te: (te[i], 0, j)),
                      pl.BlockSpec((None, d, TF), lambda i, j, te: (te[i], 0, j)),
                      pl.BlockSpec((None, TF, d), lambda i, j, te: (te[i], j, 0))],
            out_specs=pl.BlockSpec((t, d), lambda i, j, te: (i, 0)),
            scratch_shapes=[pltpu.SMEM((3, t), jnp.int32),
                            pltpu.VMEM((2, t, d), F32),
                            pltpu.SemaphoreType.DMA((3,)),
                            pltpu.SemaphoreType.DMA((2,)),
                            pltpu.VMEM((t, d), BF16),
                            pltpu.VMEM((t, d), F32)]),
        out_shape=jax.ShapeDtypeStruct((n_tiles * t, d), F32),
        compiler_params=_cparams("arbitrary", "arbitrary"),
        name="moe_experts",
    )(tile_expert, src, x, g_in.reshape(1, d), wg, wu, wd)


def _moe_combine_kernel(dest_hbm, y_hbm, tw_ref, x_ref, go_ref, o_ref, idx_ref, buf_ref, isem, gsem):
    i = pl.program_id(0)
    tm = x_ref.shape[0]

    def start_gather(step):
        for slot in range(2):
            _start_row_gather(idx_ref, step % 3, slot * tm, tm, y_hbm,
                              buf_ref.at[step % 2, slot], gsem.at[step % 2])

    _prefetch_schedule(i, dest_hbm.shape[0], dest_hbm, idx_ref, isem, start_gather)
    for slot in range(2):
        _wait_row_gather(tm, y_hbm, buf_ref.at[i % 2, slot], gsem.at[i % 2])
    tw = tw_ref[...]
    y = tw[:, 0:1] * buf_ref[i % 2, 0] + tw[:, 1:2] * buf_ref[i % 2, 1]
    o_ref[...] = x_ref[...] + _rms(y, go_ref[...])


def _moe_combine(y, dest_tiles, topw, x, g_out):
    n, d = x.shape
    return pl.pallas_call(
        _moe_combine_kernel,
        grid=(n // TM,),
        in_specs=[pl.BlockSpec(memory_space=pl.ANY),
                  pl.BlockSpec(memory_space=pl.ANY),
                  pl.BlockSpec((TM, 2), lambda i: (i, 0)),
                  pl.BlockSpec((TM, d), lambda i: (i, 0)),
                  pl.BlockSpec((1, d), lambda i: (0, 0))],
        out_specs=pl.BlockSpec((TM, d), lambda i: (i, 0)),
        out_shape=jax.ShapeDtypeStruct((n, d), F32),
        scratch_shapes=[pltpu.SMEM((3, 2 * TM), jnp.int32),
                        pltpu.VMEM((2, 2, TM, d), F32),
                        pltpu.SemaphoreType.DMA((3,)),
                        pltpu.SemaphoreType.DMA((2,))],
        compiler_params=_cparams("arbitrary"),
        name="moe_combine",
    )(dest_tiles, y, topw, x, g_out.reshape(1, d))


def _moe(x, g_in, router_w, router_b, wg, wu, wd, g_out):
    topi, topw = _router(x, g_in, router_w, router_b)
    src, tile_expert, dest_tiles = _route(topi, wg.shape[0], TM)
    y = _moe_experts(x, g_in, src, tile_expert, wg, wu, wd)
    return _moe_combine(y, dest_tiles, topw, x, g_out)


def _trunk(x, seq_len, p):
    n = x.shape[0]
    depth = p["norm_mix"].shape[0]
    for layer in range(depth):
        j = layer // 2
        if layer % 2 == 0:
            o_q = POOL_WIDTH
            o_kv = o_q + WIN_Q_HEADS * HEAD_DIM
            o_end = o_kv + 2 * WIN_KV_HEADS * HEAD_DIM
            u, q, kv = _norm_matmul(x, p["norm_mix"][layer, 0], p["w_in0"][j],
                                    ((0, o_q), (o_q, o_kv), (o_kv, o_end)), (F32, BF16, BF16))
            a = _pool_mixer(u, p["pool_w"][j], p["pool_scale"][j], seq_len)
            b = _window_attention(q, kv, p["win_bias"], p["sink"][j], seq_len)
            w_out = p["w_out0"][j]
            x = _proj_norm_res([a, b], [w_out[:POOL_WIDTH], w_out[POOL_WIDTH:]], x, p["norm_mix"][layer, 1])
            x = _ffn(x, p["norm_ffn"][layer, 0], p["ffn_wg"][j], p["ffn_wu"][j], p["ffn_wd"][j],
                     p["norm_ffn"][layer, 1])
        else:
            lambda_init = 0.8 - 0.6 * math.exp(-0.3 * layer)
            dq = DIFF_HEADS * 2 * HEAD_DIM
            q, k, v = _norm_matmul(x, p["norm_mix"][layer, 0], p["w_in1"][j],
                                   ((0, dq), (dq, 2 * dq), (2 * dq, 3 * dq)), (BF16, BF16, BF16))
            qt = q.T
            vt = v.reshape(n // TK_DIFF, TK_DIFF, dq).transpose(0, 2, 1)
            m = _diff_attention(qt, k, vt, p["diff_bias"], p["diff_lambda"][j], p["subln_g"][j],
                                seq_len, lambda_init)
            x = _proj_norm_res([m], [p["w_out1"][j]], x, p["norm_mix"][layer, 1])
            x = _moe(x, p["norm_ffn"][layer, 0], p["router_w"][j], p["router_b"][j],
                     p["moe_wg"][j], p["moe_wu"][j], p["moe_wd"][j], p["norm_ffn"][layer, 1])
    return x


def _prepare(norm_mix, norm_ffn, rel_bias, w_in0, pool_w, pool_scale, sink, w_out0,
             ffn_wg, ffn_wu, ffn_wd, w_in1, diff_lambda, subln_g, w_out1,
             router_w, router_b, moe_wg, moe_wu, moe_wd):
    scale = HEAD_DIM ** -0.5
    o_q = POOL_WIDTH
    o_kv = o_q + WIN_Q_HEADS * HEAD_DIM
    col0 = jnp.arange(w_in0.shape[-1])
    s0 = jnp.where((col0 >= o_q) & (col0 < o_kv), scale, 1.0).astype(F32)
    col1 = jnp.arange(w_in1.shape[-1])
    s1 = jnp.where(col1 < DIFF_HEADS * 2 * HEAD_DIM, scale * LOG2E, 1.0).astype(F32)
    return dict(
        norm_mix=norm_mix.astype(F32), norm_ffn=norm_ffn.astype(F32),
        win_bias=_window_bias(rel_bias, TQ_WIN), diff_bias=_diff_bias(rel_bias),
        w_in0=(w_in0 * s0).astype(BF16), pool_w=pool_w.astype(BF16), pool_scale=pool_scale.astype(F32),
        sink=sink, w_out0=w_out0.astype(BF16),
        ffn_wg=ffn_wg.astype(BF16), ffn_wu=ffn_wu.astype(BF16), ffn_wd=ffn_wd.astype(BF16),
        w_in1=(w_in1 * s1).astype(BF16), diff_lambda=diff_lambda, subln_g=subln_g,
        w_out1=w_out1.astype(BF16), router_w=router_w, router_b=router_b,
        moe_wg=moe_wg.astype(BF16), moe_wu=moe_wu.astype(BF16), moe_wd=moe_wd.astype(BF16))


def kernel(x_prompt, x_sample, norm_mix, norm_ffn, rel_bias, w_in0, pool_w, pool_scale, sink, w_out0,
           ffn_wg, ffn_wu, ffn_wd, w_in1, diff_lambda, subln_g, w_out1,
           router_w, router_b, moe_wg, moe_wu, moe_wd):
    p = _prepare(norm_mix, norm_ffn, rel_bias, w_in0, pool_w, pool_scale, sink, w_out0,
                 ffn_wg, ffn_wu, ffn_wd, w_in1, diff_lambda, subln_g, w_out1,
                 router_w, router_b, moe_wg, moe_wu, moe_wd)
    outs = []
    for x in (x_prompt, x_sample):
        b, s, d = x.shape
        outs.append(_trunk(x.reshape(b * s, d), s, p).reshape(b, s, d))
    return tuple(outs)
```

```python
import functools
import math

import jax
import jax.numpy as jnp
from jax import lax
from jax.experimental import pallas as pl
from jax.experimental.pallas import tpu as pltpu

D_MODEL = 1024
HEAD_DIM = 64
POOL_WIDTH = 512
POOL_WINDOWS = (2, 4, 8, 16)
POOL_GW = 128
POOL_HALO = 8
WIN_Q_HEADS = 8
WIN_KV_HEADS = 2
WIN_GROUP = 4
WINDOW = 128
DIFF_HEADS = 8
DIFF_VDIM = 128
NUM_BUCKETS = 32
MAX_DISTANCE = 128
D_FF = 2816
N_EXPERTS = 8
EPS = 1e-6
NEG = -1e30
LOG2E = 1.4426950408889634

LANES = 128
BF16_SUBLANES = 16
VMEM_LIMIT = 56 * 1024 * 1024

TM = 512
TF = 1408
TQ_WIN = 512
TQ_DIFF = 1024
TK_DIFF = 1024
TB_DIFF = 256
DIFF_UNROLL = 2

F32 = jnp.float32
BF16 = jnp.bfloat16


def _cparams(*sem):
    return pltpu.CompilerParams(dimension_semantics=sem, vmem_limit_bytes=VMEM_LIMIT)


def _rms(xf, g):
    ms = jnp.mean(xf * xf, axis=-1, keepdims=True)
    return xf * lax.rsqrt(ms + EPS) * g


def _t5_bucket(rel):
    half = NUM_BUCKETS // 2
    max_exact = half // 2
    ret = jnp.where(rel > 0, half, 0)
    n = jnp.abs(rel)
    nf = jnp.maximum(n, 1).astype(jnp.float32)
    large = max_exact + (jnp.log(nf / max_exact) / math.log(MAX_DISTANCE / max_exact)
                         * (half - max_exact)).astype(jnp.int32)
    large = jnp.minimum(large, half - 1)
    return ret + jnp.where(n < max_exact, n, large)


def _bias_lookup(rel_bias, rel):
    onehot = (_t5_bucket(rel)[..., None] == jnp.arange(NUM_BUCKETS)).astype(F32)
    out = jnp.einsum("...b,bh->h...", onehot, rel_bias.astype(F32), precision=lax.Precision.HIGHEST)
    return out


def _norm_matmul_kernel(x_ref, g_ref, w_ref, *o_refs, splits):
    h = _rms(x_ref[...], g_ref[...]).astype(BF16)
    z = jnp.dot(h, w_ref[...], preferred_element_type=F32)
    for o_ref, (a, b) in zip(o_refs, splits):
        o_ref[...] = z[:, a:b].astype(o_ref.dtype)


def _norm_matmul(x, g, w, splits, dtypes):
    n, d = x.shape
    nout = w.shape[1]
    return pl.pallas_call(
        functools.partial(_norm_matmul_kernel, splits=splits),
        grid=(n // TM,),
        in_specs=[pl.BlockSpec((TM, d), lambda i: (i, 0)),
                  pl.BlockSpec((1, d), lambda i: (0, 0)),
                  pl.BlockSpec((d, nout), lambda i: (0, 0))],
        out_specs=[pl.BlockSpec((TM, b - a), lambda i: (i, 0)) for a, b in splits],
        out_shape=[jax.ShapeDtypeStruct((n, b - a), dt) for (a, b), dt in zip(splits, dtypes)],
        compiler_params=_cparams("parallel"),
        name="norm_matmul",
    )(x, g.reshape(1, d), w)


def _proj_norm_res_kernel(*refs, n_in):
    ins, ws = refs[:n_in], refs[n_in:2 * n_in]
    x_ref, g_ref, o_ref = refs[2 * n_in:]
    m = jnp.dot(ins[0][...], ws[0][...], preferred_element_type=F32)
    for a_ref, w_ref in zip(ins[1:], ws[1:]):
        m = m + jnp.dot(a_ref[...], w_ref[...], preferred_element_type=F32)
    o_ref[...] = x_ref[...] + _rms(m, g_ref[...])


def _proj_norm_res(ins, ws, x, g):
    n, d = x.shape
    n_in = len(ins)
    return pl.pallas_call(
        functools.partial(_proj_norm_res_kernel, n_in=n_in),
        grid=(n // TM,),
        in_specs=([pl.BlockSpec((TM, a.shape[1]), lambda i: (i, 0)) for a in ins]
                  + [pl.BlockSpec(w.shape, lambda i: (0, 0)) for w in ws]
                  + [pl.BlockSpec((TM, d), lambda i: (i, 0)),
                     pl.BlockSpec((1, d), lambda i: (0, 0))]),
        out_specs=pl.BlockSpec((TM, d), lambda i: (i, 0)),
        out_shape=jax.ShapeDtypeStruct((n, d), F32),
        compiler_params=_cparams("parallel"),
        name="proj_norm_res",
    )(*ins, *ws, x, g.reshape(1, d))


def _pool_kernel(up_ref, u_ref, un_ref, w_ref, sc_ref, o_ref, ext_ref, *, seq_len):
    tm = u_ref.shape[0]
    t0 = (pl.program_id(0) * tm) % seq_len
    ext_ref[0:POOL_HALO, :] = jnp.where(t0 > 0, up_ref[...], 0.0)
    ext_ref[POOL_HALO:POOL_HALO + tm, :] = u_ref[...]
    ext_ref[POOL_HALO + tm:, :] = jnp.where(t0 + tm < seq_len, un_ref[...], 0.0)
    t = t0 + lax.broadcasted_iota(jnp.int32, (tm, 1), 0)
    outs = []
    for g, w in enumerate(POOL_WINDOWS):
        half = w // 2
        sl = slice(g * POOL_GW, (g + 1) * POOL_GW)
        acc = ext_ref[POOL_HALO - half:POOL_HALO - half + tm, sl]
        for j in range(-half + 1, half):
            acc = acc + ext_ref[POOL_HALO + j:POOL_HALO + j + tm, sl]
        lo = jnp.maximum(t - half, 0)
        hi = jnp.minimum(t + half - 1, seq_len - 1)
        cnt = (hi - lo + 1).astype(F32)
        d = (acc / cnt - u_ref[:, sl]).astype(BF16)
        outs.append(jnp.dot(d, w_ref[g], preferred_element_type=F32))
    o_ref[...] = (jnp.concatenate(outs, axis=-1) * sc_ref[...]).astype(o_ref.dtype)


def _pool_mixer(u, pool_w, pool_scale, seq_len):
    n, c = u.shape
    hb = TM // POOL_HALO
    last = n // POOL_HALO - 1
    return pl.pallas_call(
        functools.partial(_pool_kernel, seq_len=seq_len),
        grid=(n // TM,),
        in_specs=[pl.BlockSpec((POOL_HALO, c), lambda i: (jnp.maximum(i * hb - 1, 0), 0)),
                  pl.BlockSpec((TM, c), lambda i: (i, 0)),
                  pl.BlockSpec((POOL_HALO, c), lambda i: (jnp.minimum((i + 1) * hb, last), 0)),
                  pl.BlockSpec(pool_w.shape, lambda i: (0, 0, 0)),
                  pl.BlockSpec((1, c), lambda i: (0, 0))],
        out_specs=pl.BlockSpec((TM, c), lambda i: (i, 0)),
        out_shape=jax.ShapeDtypeStruct((n, c), BF16),
        scratch_shapes=[pltpu.VMEM((TM + 2 * POOL_HALO, c), F32)],
        compiler_params=_cparams("parallel"),
        name="pool_mixer",
    )(u, u, u, pool_w, pool_scale.reshape(1, c))


def _win_kernel(q_ref, kvp_ref, kv_ref, kvn_ref, bias_ref, sink_ref, o_ref, *, seq_len):
    tq = q_ref.shape[0]
    nk = tq + 2 * WINDOW
    t0 = (pl.program_id(0) * tq) % seq_len
    kv = jnp.concatenate([kvp_ref[...], kv_ref[...], kvn_ref[...]], axis=0)
    kpos = t0 - WINDOW + lax.broadcasted_iota(jnp.int32, (1, nk), 1)
    valid = (kpos >= 0) & (kpos < seq_len)
    q = q_ref[...]
    sink = sink_ref[...]
    heads = range(WIN_Q_HEADS)
    scores = []
    for h in heads:
        g = h // WIN_GROUP
        qh = q[:, h * HEAD_DIM:(h + 1) * HEAD_DIM]
        k = kv[:, g * HEAD_DIM:(g + 1) * HEAD_DIM]
        s = lax.dot_general(qh, k, (((1,), (1,)), ((), ())), preferred_element_type=F32)
        scores.append(jnp.where(valid, s + bias_ref[h], NEG))
    probs = []
    for h in heads:
        s = scores[h]
        sk = sink[:, h:h + 1]
        m = jnp.maximum(jnp.max(s, axis=-1, keepdims=True), sk)
        e = jnp.exp(s - m)
        l = jnp.sum(e, axis=-1, keepdims=True) + jnp.exp(sk - m)
        probs.append((e / l).astype(BF16))
    outs = []
    for h in heads:
        g = h // WIN_GROUP
        v = kv[:, (WIN_KV_HEADS + g) * HEAD_DIM:(WIN_KV_HEADS + g + 1) * HEAD_DIM]
        outs.append(jnp.dot(probs[h], v, preferred_element_type=F32))
    o_ref[...] = jnp.concatenate(outs, axis=-1).astype(o_ref.dtype)


def _window_bias(rel_bias, tq):
    nk = tq + 2 * WINDOW
    rel = jnp.arange(nk)[None, :] - WINDOW - jnp.arange(tq)[:, None]
    return jnp.where((jnp.abs(rel) <= WINDOW)[None], _bias_lookup(rel_bias, rel), NEG)


def _window_attention(q, kv, bias, sink, seq_len):
    n = q.shape[0]
    tq = TQ_WIN
    hb = tq // WINDOW
    last = n // WINDOW - 1
    return pl.pallas_call(
        functools.partial(_win_kernel, seq_len=seq_len),
        grid=(n // tq,),
        in_specs=[pl.BlockSpec((tq, q.shape[1]), lambda i: (i, 0)),
                  pl.BlockSpec((WINDOW, kv.shape[1]), lambda i: (jnp.maximum(i * hb - 1, 0), 0)),
                  pl.BlockSpec((tq, kv.shape[1]), lambda i: (i, 0)),
                  pl.BlockSpec((WINDOW, kv.shape[1]), lambda i: (jnp.minimum((i + 1) * hb, last), 0)),
                  pl.BlockSpec(bias.shape, lambda i: (0, 0, 0)),
                  pl.BlockSpec((1, WIN_Q_HEADS), lambda i: (0, 0))],
        out_specs=pl.BlockSpec((tq, q.shape[1]), lambda i: (i, 0)),
        out_shape=jax.ShapeDtypeStruct(q.shape, BF16),
        compiler_params=_cparams("parallel"),
        name="window_attention",
    )(q, kv, kv, kv, bias, sink.reshape(1, WIN_Q_HEADS).astype(F32))


def _ffn_kernel(x_ref, gi_ref, wg_ref, wu_ref, wd_ref, go_ref, o_ref, h_ref, acc_ref):
    j = pl.program_id(1)

    @pl.when(j == 0)
    def _():
        h_ref[...] = _rms(x_ref[...], gi_ref[...]).astype(BF16)
        acc_ref[...] = jnp.zeros_like(acc_ref)

    h = h_ref[...]
    a = jnp.dot(h, wg_ref[...], preferred_element_type=F32)
    b = jnp.dot(h, wu_ref[...], preferred_element_type=F32)
    mid = (a * jax.nn.sigmoid(a) * b).astype(BF16)
    acc_ref[...] += jnp.dot(mid, wd_ref[...], preferred_element_type=F32)

    @pl.when(j == pl.num_programs(1) - 1)
    def _():
        o_ref[...] = x_ref[...] + _rms(acc_ref[...], go_ref[...])


def _ffn(x, g_in, wg, wu, wd, g_out):
    n, d = x.shape
    nf = wg.shape[1] // TF
    return pl.pallas_call(
        _ffn_kernel,
        grid=(n // TM, nf),
        in_specs=[pl.BlockSpec((TM, d), lambda i, j: (i, 0)),
                  pl.BlockSpec((1, d), lambda i, j: (0, 0)),
                  pl.BlockSpec((d, TF), lambda i, j: (0, j)),
                  pl.BlockSpec((d, TF), lambda i, j: (0, j)),
                  pl.BlockSpec((TF, d), lambda i, j: (j, 0)),
                  pl.BlockSpec((1, d), lambda i, j: (0, 0))],
        out_specs=pl.BlockSpec((TM, d), lambda i, j: (i, 0)),
        out_shape=jax.ShapeDtypeStruct((n, d), F32),
        scratch_shapes=[pltpu.VMEM((TM, d), BF16), pltpu.VMEM((TM, d), F32)],
        compiler_params=_cparams("parallel", "arbitrary"),
        name="ffn",
    )(x, g_in.reshape(1, d), wg, wu, wd, g_out.reshape(1, d))


def _diff_kernel(qt_ref, k_ref, vt_ref, bias_ref, lam_ref, g_ref, o_ref,
                 qbd_ref, s0_ref, s1_ref, e0_ref, e1_ref, acc_ref, *, lambda_init):
    tq = qt_ref.shape[1]
    nk, _, tk = vt_ref.shape
    i = pl.program_id(2)
    qt = qt_ref[...]
    zero = jnp.zeros((HEAD_DIM, tq), BF16)
    qbd_ref[0:HEAD_DIM, 0:tq] = qt[0:HEAD_DIM]
    qbd_ref[0:HEAD_DIM, tq:] = zero
    qbd_ref[HEAD_DIM:, 0:tq] = zero
    qbd_ref[HEAD_DIM:, tq:] = qt[HEAD_DIM:]
    acc_ref[...] = jnp.zeros_like(acc_ref)
    ones = jnp.ones((BF16_SUBLANES, tk), BF16)

    n_near = 3
    base = jnp.clip(i - 1, 0, nk - n_near)
    bias_left = bias_ref[0, 0:1, 0:1].astype(F32)
    bias_right = bias_ref[4, 0:1, 0:1].astype(F32)

    def chunk_at(p):
        far = p - n_near
        return jnp.where(p < n_near, base + p, jnp.where(far < base, far, far + n_near))

    def scores_far(j, dst_ref):
        kc = k_ref[pl.ds(pl.multiple_of(j * tk, tk), tk), :]
        s = jnp.dot(kc, qbd_ref[...], preferred_element_type=F32).astype(BF16)
        dst_ref[...] = s
        off = jnp.where(j < i, bias_left, bias_right)
        return jnp.max(s, axis=0, keepdims=True).astype(F32) + off, off

    def scores(j, dst_ref):
        kc = k_ref[pl.ds(pl.multiple_of(j * tk, tk), tk), :]
        s = jnp.dot(kc, qbd_ref[...], preferred_element_type=F32).astype(BF16)
        rows = []
        for c in range(tk // TB_DIFF):
            sc = s[c * TB_DIFF:(c + 1) * TB_DIFF]
            cols = []
            for mp in range(2):
                for a in range(tq // TB_DIFF):
                    d = jnp.clip(j * (tk // TB_DIFF) + c - (i * (tq // TB_DIFF) + a), -2, 2) + 2
                    c0 = mp * tq + a * TB_DIFF
                    cols.append(sc[:, c0:c0 + TB_DIFF] + bias_ref[d])
            rows.append(jnp.concatenate(cols, axis=1))
        s = jnp.concatenate(rows, axis=0)
        dst_ref[...] = s
        return jnp.max(s, axis=0, keepdims=True).astype(F32), jnp.zeros((1, 1), F32)

    def softmax(src_ref, dst_ref, cmax, off, m_old):
        m_new = jnp.maximum(m_old, cmax)
        alpha = jnp.exp2(m_old - m_new)
        dst_ref[...] = jnp.exp2(src_ref[...] - (m_new - off).astype(BF16))
        return m_new, alpha

    def accumulate(j, src_ref, alpha):
        vt = jnp.concatenate([vt_ref[j], ones], axis=0)
        acc_ref[...] = alpha * acc_ref[...] + jnp.dot(vt, src_ref[...], preferred_element_type=F32)

    s_refs = (s0_ref, s1_ref)
    e_refs = (e0_ref, e1_ref)

    def step(p, parity, next_scores, carry):
        m, a_prev, c_cur, off_cur = carry
        c_next, off_next = next_scores(chunk_at(p + 1), s_refs[1 - parity])
        m, a_cur = softmax(s_refs[parity], e_refs[parity], c_cur, off_cur, m)
        accumulate(chunk_at(p - 1), e_refs[1 - parity], a_prev)
        return m, a_cur, c_next, off_next

    m = jnp.full((1, 2 * tq), NEG, F32)
    c0, off0 = scores(base, s0_ref)
    c1, off1 = scores(base + 1, s1_ref)
    m, a0 = softmax(s0_ref, e0_ref, c0, off0, m)
    carry = step(1, 1, scores, (m, a0, c1, off1))

    n_trips = (nk - n_near) // DIFF_UNROLL

    def trip(jj, carry):
        for u in range(DIFF_UNROLL):
            carry = step(DIFF_UNROLL * jj + 2 + u, u % 2, scores_far, carry)
        return carry

    carry = lax.fori_loop(0, n_trips, trip, carry)
    for p in range(2 + DIFF_UNROLL * n_trips, nk - 1):
        carry = step(p, p % 2, scores_far, carry)
    m, a_prev, c_last, off_last = carry
    last = (nk - 1) % 2
    m, a_last = softmax(s_refs[last], e_refs[last], c_last, off_last, m)
    accumulate(chunk_at(nk - 2), e_refs[1 - last], a_prev)
    accumulate(chunk_at(nk - 1), e_refs[last], a_last)

    acc = acc_ref[...]
    o = acc[:DIFF_VDIM] / acc[DIFF_VDIM:DIFF_VDIM + 1]
    lp = lam_ref[...]
    lam = (jnp.exp(jnp.sum(lp[0:1] * lp[1:2], axis=-1, keepdims=True))
           - jnp.exp(jnp.sum(lp[2:3] * lp[3:4], axis=-1, keepdims=True)) + lambda_init)
    out = o[:, :tq] - lam * o[:, tq:]
    ms = jnp.mean(out * out, axis=0, keepdims=True)
    y = out * lax.rsqrt(ms + EPS) * g_ref[...] * (1.0 - lambda_init)
    o_ref[...] = y.T.astype(o_ref.dtype)


def _diff_bias(rel_bias):
    t = TB_DIFF
    assert t >= MAX_DISTANCE
    kk = jnp.arange(t)[:, None]
    qq = jnp.arange(t)[None, :]
    rel = jnp.stack([d * t + kk - qq for d in range(-2, 3)])
    return (_bias_lookup(rel_bias, rel) * LOG2E).astype(BF16)


def _diff_attention(qt, k, vt, bias, diff_lambda, subln_g, seq_len, lambda_init):
    n = k.shape[0]
    tq, tk = TQ_DIFF, TK_DIFF
    assert tq == tk >= MAX_DISTANCE and seq_len % tk == 0 and seq_len // tk >= 3 and DIFF_UNROLL % 2 == 0
    assert tq % TB_DIFF == 0 and tk % TB_DIFF == 0
    nb = n // seq_len
    nq = seq_len // tq
    nk = seq_len // tk
    return pl.pallas_call(
        functools.partial(_diff_kernel, lambda_init=lambda_init),
        grid=(nb, DIFF_HEADS, nq),
        in_specs=[pl.BlockSpec((DIFF_VDIM, tq), lambda b, h, i: (h, b * nq + i)),
                  pl.BlockSpec((seq_len, DIFF_VDIM), lambda b, h, i: (b, h)),
                  pl.BlockSpec((nk, DIFF_VDIM, tk), lambda b, h, i: (b, h, 0)),
                  pl.BlockSpec((None, 5, TB_DIFF, TB_DIFF), lambda b, h, i: (h, 0, 0, 0)),
                  pl.BlockSpec((4, HEAD_DIM), lambda b, h, i: (0, 0)),
                  pl.BlockSpec((DIFF_VDIM, 1), lambda b, h, i: (0, 0))],
        out_specs=pl.BlockSpec((tq, DIFF_VDIM), lambda b, h, i: (b * nq + i, h)),
        out_shape=jax.ShapeDtypeStruct((n, DIFF_HEADS * DIFF_VDIM), BF16),
        scratch_shapes=[pltpu.VMEM((DIFF_VDIM, 2 * tq), BF16),
                        pltpu.VMEM((tk, 2 * tq), BF16),
                        pltpu.VMEM((tk, 2 * tq), BF16),
                        pltpu.VMEM((tk, 2 * tq), BF16),
                        pltpu.VMEM((tk, 2 * tq), BF16),
                        pltpu.VMEM((DIFF_VDIM + BF16_SUBLANES, 2 * tq), F32)],
        compiler_params=_cparams("parallel", "parallel", "arbitrary"),
        name="diff_attention",
    )(qt, k, vt, bias, diff_lambda.astype(F32), subln_g.reshape(DIFF_VDIM, 1).astype(F32))


def _router_kernel(x_ref, gi_ref, rw_ref, rb_ref, ti_ref, tw_ref):
    h = _rms(x_ref[...], gi_ref[...]).astype(BF16)
    logits = jnp.dot(h, rw_ref[...], preferred_element_type=F32) + rb_ref[...]
    col = lax.broadcasted_iota(jnp.int32, logits.shape, 1)
    m1 = jnp.max(logits, axis=-1, keepdims=True)
    i1 = jnp.min(jnp.where(logits == m1, col, LANES), axis=-1, keepdims=True)
    rest = jnp.where(col == i1, 2 * NEG, logits)
    m2 = jnp.max(rest, axis=-1, keepdims=True)
    i2 = jnp.min(jnp.where(rest == m2, col, LANES), axis=-1, keepdims=True)
    e2 = jnp.exp(m2 - m1)
    den = 1.0 + e2
    ti_ref[...] = jnp.concatenate([i1, i2], axis=1)
    tw_ref[...] = jnp.concatenate([1.0 / den, e2 / den], axis=1)


def _router(x, g_in, router_w, router_b):
    n, d = x.shape
    ne = router_w.shape[1]
    rw = jnp.zeros((d, LANES), BF16).at[:, :ne].set(router_w.astype(BF16))
    rb = jnp.full((1, LANES), NEG, F32).at[0, :ne].set(router_b.astype(F32))
    return pl.pallas_call(
        _router_kernel,
        grid=(n // TM,),
        in_specs=[pl.BlockSpec((TM, d), lambda i: (i, 0)),
                  pl.BlockSpec((1, d), lambda i: (0, 0)),
                  pl.BlockSpec((d, LANES), lambda i: (0, 0)),
                  pl.BlockSpec((1, LANES), lambda i: (0, 0))],
        out_specs=[pl.BlockSpec((TM, 2), lambda i: (i, 0)), pl.BlockSpec((TM, 2), lambda i: (i, 0))],
        out_shape=[jax.ShapeDtypeStruct((n, 2), jnp.int32), jax.ShapeDtypeStruct((n, 2), F32)],
        compiler_params=_cparams("parallel"),
        name="moe_router",
    )(x, g_in.reshape(1, d), rw, rb)


def _proj_router_kernel(a_ref, w_ref, x_ref, g_ref, gi_ref, rw_ref, rb_ref, o_ref, ti_ref, tw_ref):
    m = jnp.dot(a_ref[...], w_ref[...], preferred_element_type=F32)
    x_new = x_ref[...] + _rms(m, g_ref[...])
    o_ref[...] = x_new
    _router_kernel(o_ref, gi_ref, rw_ref, rb_ref, ti_ref, tw_ref)


def _proj_norm_res_router(a, w, x, g, g_in, router_w, router_b):
    n, d = x.shape
    ne = router_w.shape[1]
    rw = jnp.zeros((d, LANES), BF16).at[:, :ne].set(router_w.astype(BF16))
    rb = jnp.full((1, LANES), NEG, F32).at[0, :ne].set(router_b.astype(F32))
    return pl.pallas_call(
        _proj_router_kernel,
        grid=(n // TM,),
        in_specs=[pl.BlockSpec((TM, a.shape[1]), lambda i: (i, 0)),
                  pl.BlockSpec(w.shape, lambda i: (0, 0)),
                  pl.BlockSpec((TM, d), lambda i: (i, 0)),
                  pl.BlockSpec((1, d), lambda i: (0, 0)),
                  pl.BlockSpec((1, d), lambda i: (0, 0)),
                  pl.BlockSpec((d, LANES), lambda i: (0, 0)),
                  pl.BlockSpec((1, LANES), lambda i: (0, 0))],
        out_specs=[pl.BlockSpec((TM, d), lambda i: (i, 0)),
                   pl.BlockSpec((TM, 2), lambda i: (i, 0)), pl.BlockSpec((TM, 2), lambda i: (i, 0))],
        out_shape=[jax.ShapeDtypeStruct((n, d), F32),
                   jax.ShapeDtypeStruct((n, 2), jnp.int32), jax.ShapeDtypeStruct((n, 2), F32)],
        compiler_params=_cparams("parallel"),
        name="proj_norm_res_router",
    )(a, w, x, g.reshape(1, d), g_in.reshape(1, d), rw, rb)


def _route(topi, ne, t):
    n = topi.shape[0]
    p = 2 * n
    e = topi.reshape(p)
    experts = jnp.arange(ne, dtype=jnp.int32)
    onehot = (e[:, None] == experts[None, :]).astype(jnp.int32)
    csum = jnp.cumsum(onehot, axis=0)
    rank = jnp.sum(onehot * csum, axis=1) - 1
    cnt = csum[-1]
    padded = ((cnt + t - 1) // t) * t
    ends = jnp.cumsum(padded)
    starts = ends - padded
    dest = jnp.sum(onehot * starts[None, :], axis=1) + rank
    n_tiles = p // t + ne
    tile_expert = jnp.sum(jnp.arange(n_tiles, dtype=jnp.int32)[:, None] * t >= ends[None, :], axis=1)
    tile_expert = jnp.minimum(tile_expert, ne - 1).astype(jnp.int32)
    _, tok_sorted = lax.sort_key_val(dest.astype(jnp.int32), jnp.arange(p, dtype=jnp.int32) // 2)
    pad_before = starts - (jnp.cumsum(cnt) - cnt)
    row_expert = jnp.repeat(tile_expert, t)
    row_shift = jnp.sum((row_expert[:, None] == experts[None, :]) * pad_before[None, :], axis=1)
    src = jnp.take(tok_sorted, jnp.clip(jnp.arange(n_tiles * t, dtype=jnp.int32) - row_shift, 0, p - 1))
    dest_tiles = dest.reshape(n // TM, TM, 2).transpose(0, 2, 1).reshape(n // TM, 2 * TM)
    return src.reshape(n_tiles, t), tile_expert, dest_tiles.astype(jnp.int32)


def _row_copy(src_hbm, row, dst_ref, r, sem):
    return pltpu.make_async_copy(src_hbm.at[pl.ds(row, 1)], dst_ref.at[pl.ds(r, 1)], sem)


def _start_row_gather(idx_ref, islot, base, n_rows, src_hbm, dst_ref, sem):
    def body(r, c):
        _row_copy(src_hbm, idx_ref[islot, base + r], dst_ref, r, sem).start()
        return c
    lax.fori_loop(0, n_rows, body, 0, unroll=8)


def _wait_row_gather(n_rows, src_hbm, dst_ref, sem):
    def body(r, c):
        _row_copy(src_hbm, 0, dst_ref, 0, sem).wait()
        return c
    lax.fori_loop(0, n_rows, body, 0, unroll=8)


def _prefetch_schedule(t, n_steps, idx_hbm, idx_ref, isem, start_gather):
    def idx_copy(step):
        return pltpu.make_async_copy(idx_hbm.at[step], idx_ref.at[step % 3], isem.at[step % 3])

    @pl.when(t == 0)
    def _():
        idx_copy(0).start()
        idx_copy(0).wait()
        start_gather(0)
        if n_steps > 1:
            idx_copy(1).start()

    @pl.when(t + 1 < n_steps)
    def _():
        idx_copy(t + 1).wait()

        @pl.when(t + 2 < n_steps)
        def _():
            idx_copy(t + 2).start()

        start_gather(t + 1)


def _moe_expert_kernel(te_ref, src_hbm, x_hbm, gi_ref, wg_ref, wu_ref, wd_ref, o_ref,
                       idx_ref, xg_ref, isem, gsem, h_ref, acc_ref):
    t = pl.program_id(0)
    j = pl.program_id(1)
    rows = xg_ref.shape[1]

    @pl.when(j == 0)
    def _():
        def start_gather(step):
            _start_row_gather(idx_ref, step % 3, 0, rows, x_hbm, xg_ref.at[step % 2], gsem.at[step % 2])

        _prefetch_schedule(t, src_hbm.shape[0], src_hbm, idx_ref, isem, start_gather)
        _wait_row_gather(rows, x_hbm, xg_ref.at[t % 2], gsem.at[t % 2])
        h_ref[...] = _rms(xg_ref[t % 2], gi_ref[...]).astype(BF16)
        acc_ref[...] = jnp.zeros_like(acc_ref)

    h = h_ref[...]
    a = jnp.dot(h, wg_ref[...], preferred_element_type=F32)
    b = jnp.dot(h, wu_ref[...], preferred_element_type=F32)
    mid = (a * jax.nn.sigmoid(a) * b).astype(BF16)
    acc_ref[...] += jnp.dot(mid, wd_ref[...], preferred_element_type=F32)

    @pl.when(j == pl.num_programs(1) - 1)
    def _():
        o_ref[...] = acc_ref[...]


def _moe_experts(x, g_in, src, tile_expert, wg, wu, wd):
    n, d = x.shape
    n_tiles, t = src.shape
    nf = wg.shape[2] // TF
    return pl.pallas_call(
        _moe_expert_kernel,
        grid_spec=pltpu.PrefetchScalarGridSpec(
            num_scalar_prefetch=1,
            grid=(n_tiles, nf),
            in_specs=[pl.BlockSpec(memory_space=pl.ANY),
                      pl.BlockSpec(memory_space=pl.ANY),
                      pl.BlockSpec((1, d), lambda i, j, te: (0, 0)),
                      pl.BlockSpec((None, d, TF), lambda i, j, te: (te[i], 0, j)),
                      pl.BlockSpec((None, d, TF), lambda i, j, te: (te[i], 0, j)),
                      pl.BlockSpec((None, TF, d), lambda i, j, te: (te[i], j, 0))],
            out_specs=pl.BlockSpec((t, d), lambda i, j, te: (i, 0)),
            scratch_shapes=[pltpu.SMEM((3, t), jnp.int32),
                            pltpu.VMEM((2, t, d), F32),
                            pltpu.SemaphoreType.DMA((3,)),
                            pltpu.SemaphoreType.DMA((2,)),
                            pltpu.VMEM((t, d), BF16),
                            pltpu.VMEM((t, d), F32)]),
        out_shape=jax.ShapeDtypeStruct((n_tiles * t, d), F32),
        compiler_params=_cparams("arbitrary", "arbitrary"),
        name="moe_experts",
    )(tile_expert, src, x, g_in.reshape(1, d), wg, wu, wd)


def _moe_combine_kernel(dest_hbm, y_hbm, tw_ref, x_ref, go_ref, o_ref, idx_ref, buf_ref, isem, gsem):
    i = pl.program_id(0)
    tm = x_ref.shape[0]

    def start_gather(step):
        for slot in range(2):
            _start_row_gather(idx_ref, step % 3, slot * tm, tm, y_hbm,
                              buf_ref.at[step % 2, slot], gsem.at[step % 2])

    _prefetch_schedule(i, dest_hbm.shape[0], dest_hbm, idx_ref, isem, start_gather)
    for slot in range(2):
        _wait_row_gather(tm, y_hbm, buf_ref.at[i % 2, slot], gsem.at[i % 2])
    tw = tw_ref[...]
    y = tw[:, 0:1] * buf_ref[i % 2, 0] + tw[:, 1:2] * buf_ref[i % 2, 1]
    o_ref[...] = x_ref[...] + _rms(y, go_ref[...])


def _moe_combine(y, dest_tiles, topw, x, g_out):
    n, d = x.shape
    return pl.pallas_call(
        _moe_combine_kernel,
        grid=(n // TM,),
        in_specs=[pl.BlockSpec(memory_space=pl.ANY),
                  pl.BlockSpec(memory_space=pl.ANY),
                  pl.BlockSpec((TM, 2), lambda i: (i, 0)),
                  pl.BlockSpec((TM, d), lambda i: (i, 0)),
                  pl.BlockSpec((1, d), lambda i: (0, 0))],
        out_specs=pl.BlockSpec((TM, d), lambda i: (i, 0)),
        out_shape=jax.ShapeDtypeStruct((n, d), F32),
        scratch_shapes=[pltpu.SMEM((3, 2 * TM), jnp.int32),
                        pltpu.VMEM((2, 2, TM, d), F32),
                        pltpu.SemaphoreType.DMA((3,)),
                        pltpu.SemaphoreType.DMA((2,))],
        compiler_params=_cparams("arbitrary"),
        name="moe_combine",
    )(dest_tiles, y, topw, x, g_out.reshape(1, d))


def _moe(x, g_in, topi, topw, wg, wu, wd, g_out):
    src, tile_expert, dest_tiles = _route(topi, wg.shape[0], TM)
    y = _moe_experts(x, g_in, src, tile_expert, wg, wu, wd)
    return _moe_combine(y, dest_tiles, topw, x, g_out)


def _trunk(x, seq_len, p):
    n = x.shape[0]
    depth = p["norm_mix"].shape[0]
    for layer in range(depth):
        j = layer // 2
        if layer % 2 == 0:
            o_q = POOL_WIDTH
            o_kv = o_q + WIN_Q_HEADS * HEAD_DIM
            o_end = o_kv + 2 * WIN_KV_HEADS * HEAD_DIM
            u, q, kv = _norm_matmul(x, p["norm_mix"][layer, 0], p["w_in0"][j],
                                    ((0, o_q), (o_q, o_kv), (o_kv, o_end)), (F32, BF16, BF16))
            a = _pool_mixer(u, p["pool_w"][j], p["pool_scale"][j], seq_len)
            b = _window_attention(q, kv, p["win_bias"], p["sink"][j], seq_len)
            w_out = p["w_out0"][j]
            x = _proj_norm_res([a, b], [w_out[:POOL_WIDTH], w_out[POOL_WIDTH:]], x, p["norm_mix"][layer, 1])
            x = _ffn(x, p["norm_ffn"][layer, 0], p["ffn_wg"][j], p["ffn_wu"][j], p["ffn_wd"][j],
                     p["norm_ffn"][layer, 1])
        else:
            lambda_init = 0.8 - 0.6 * math.exp(-0.3 * layer)
            dq = DIFF_HEADS * 2 * HEAD_DIM
            q, k, v = _norm_matmul(x, p["norm_mix"][layer, 0], p["w_in1"][j],
                                   ((0, dq), (dq, 2 * dq), (2 * dq, 3 * dq)), (BF16, BF16, BF16))
            qt = q.T
            vt = v.reshape(n // TK_DIFF, TK_DIFF, dq).transpose(0, 2, 1)
            m = _diff_attention(qt, k, vt, p["diff_bias"], p["diff_lambda"][j], p["subln_g"][j],
                                seq_len, lambda_init)
            x, topi, topw = _proj_norm_res_router(m, p["w_out1"][j], x, p["norm_mix"][layer, 1],
                                                  p["norm_ffn"][layer, 0], p["router_w"][j], p["router_b"][j])
            x = _moe(x, p["norm_ffn"][layer, 0], topi, topw,
                     p["moe_wg"][j], p["moe_wu"][j], p["moe_wd"][j], p["norm_ffn"][layer, 1])
    return x


def _prepare(norm_mix, norm_ffn, rel_bias, w_in0, pool_w, pool_scale, sink, w_out0,
             ffn_wg, ffn_wu, ffn_wd, w_in1, diff_lambda, subln_g, w_out1,
             router_w, router_b, moe_wg, moe_wu, moe_wd):
    scale = HEAD_DIM ** -0.5
    o_q = POOL_WIDTH
    o_kv = o_q + WIN_Q_HEADS * HEAD_DIM
    col0 = jnp.arange(w_in0.shape[-1])
    s0 = jnp.where((col0 >= o_q) & (col0 < o_kv), scale, 1.0).astype(F32)
    col1 = jnp.arange(w_in1.shape[-1])
    s1 = jnp.where(col1 < DIFF_HEADS * 2 * HEAD_DIM, scale * LOG2E, 1.0).astype(F32)
    return dict(
        norm_mix=norm_mix.astype(F32), norm_ffn=norm_ffn.astype(F32),
        win_bias=_window_bias(rel_bias, TQ_WIN), diff_bias=_diff_bias(rel_bias),
        w_in0=(w_in0 * s0).astype(BF16), pool_w=pool_w.astype(BF16), pool_scale=pool_scale.astype(F32),
        sink=sink, w_out0=w_out0.astype(BF16),
        ffn_wg=ffn_wg.astype(BF16), ffn_wu=ffn_wu.astype(BF16), ffn_wd=ffn_wd.astype(BF16),
        w_in1=(w_in1 * s1).astype(BF16), diff_lambda=diff_lambda, subln_g=subln_g,
        w_out1=w_out1.astype(BF16), router_w=router_w, router_b=router_b,
        moe_wg=moe_wg.astype(BF16), moe_wu=moe_wu.astype(BF16), moe_wd=moe_wd.astype(BF16))


def kernel(x_prompt, x_sample, norm_mix, norm_ffn, rel_bias, w_in0, pool_w, pool_scale, sink, w_out0,
           ffn_wg, ffn_wu, ffn_wd, w_in1, diff_lambda, subln_g, w_out1,
           router_w, router_b, moe_wg, moe_wu, moe_wd):
    p = _prepare(norm_mix, norm_ffn, rel_bias, w_in0, pool_w, pool_scale, sink, w_out0,
                 ffn_wg, ffn_wu, ffn_wd, w_in1, diff_lambda, subln_g, w_out1,
                 router_w, router_b, moe_wg, moe_wu, moe_wd)
    outs = []
    for x in (x_prompt, x_sample):
        b, s, d = x.shape
        outs.append(_trunk(x.reshape(b * s, d), s, p).reshape(b, s, d))
    return tuple(outs)
```
